```python
import math, functools
import jax, jax.numpy as jnp
from jax import lax
import numpy as np

D_MODEL = 1024
BATCH = 4
SEQ = 8192
DEPTH = 1
DEC_BATCH = 32
DEC_SEQ = 8
PAST_LEN = 16384
PAGE_SIZE = 128

H_RET = 4
DK_RET = 64
DV_RET = 128
RET_CHUNK = 128
H_ATT = 8
HD_ATT = 64
DILATED_PAIRS = ((128, 1), (512, 4), (2048, 16))
ATT_BLOCK = 128
MAX_WINDOW = 2048
ROPE_THETA = 10000.0
RET_QK = H_RET * DK_RET
RET_V = H_RET * DV_RET
ATT_W = H_ATT * HD_ATT
MIX_WIDTH = RET_V + ATT_W
IN_WIDTH = 2 * RET_QK + 2 * RET_V + 3 * ATT_W
N_EXPERTS = 32
TOP_K = 4
D_FF = 1024
SWIGLU_LIMIT = 7.0
SWIGLU_ALPHA = 1.702
MOE_BLOCK = 128
N_MOD = 6
EPS = 1e-6

kernel_name = "hymba_retention_dilated_moe_adaln_step"


def rmsnorm(x, g):
    xf = x.astype(jnp.float32)
    y = xf * lax.rsqrt(jnp.mean(xf * xf, axis=-1, keepdims=True) + EPS)
    return (y * g.astype(jnp.float32)).astype(x.dtype)


def adaln(c, w, b, n):
    mod = jax.nn.silu(c) @ w + b
    return jnp.split(mod[:, None, :], n, axis=-1)


def modulate(h, shift, scale):
    return (h * (1.0 + scale) + shift).astype(h.dtype)


def rope(x, pos):
    half = x.shape[-1] // 2
    inv = ROPE_THETA ** (-jnp.arange(half, dtype=jnp.float32) / half)
    ang = pos.astype(jnp.float32)[:, None] * inv[None, :]
    cos = jnp.cos(ang)[None, :, None, :]
    sin = jnp.sin(ang)[None, :, None, :]
    xf = x.astype(jnp.float32)
    x1, x2 = xf[..., :half], xf[..., half:]
    return jnp.concatenate([x1 * cos - x2 * sin, x1 * sin + x2 * cos], axis=-1).astype(x.dtype)


def project(h, w_in, pos):
    B, L, _ = h.shape
    sizes = (RET_QK, RET_QK, RET_V, RET_V, ATT_W, ATT_W, ATT_W)
    points = [int(v) for v in np.cumsum(sizes)[:-1]]
    qr, kr, vr, gr, qa, ka, va = jnp.split(h @ w_in, points, axis=-1)
    qr = rope(qr.reshape(B, L, H_RET, DK_RET), pos)
    kr = (rope(kr.reshape(B, L, H_RET, DK_RET), pos) * (DK_RET ** -0.5)).astype(h.dtype)
    vr = vr.reshape(B, L, H_RET, DV_RET)
    qa = rope(qa.reshape(B, L, H_ATT, HD_ATT), pos)
    ka = rope(ka.reshape(B, L, H_ATT, HD_ATT), pos)
    va = va.reshape(B, L, H_ATT, HD_ATT)
    return qr, kr, vr, gr, qa, ka, va


def ret_log_decay():
    return jnp.log(1.0 - 2.0 ** (-5.0 - jnp.arange(H_RET, dtype=jnp.float32)))


def retention_chunk(S, q, k, v):
    L = q.shape[1]
    lg = ret_log_decay()
    qf, kf, vf = q.astype(jnp.float32), k.astype(jnp.float32), v.astype(jnp.float32)
    idx = jnp.arange(L, dtype=jnp.float32)
    diff = idx[:, None] - idx[None, :]
    causal = diff >= 0
    decay = jnp.where(causal[None], jnp.exp(jnp.where(causal, diff, 0.0)[None] * lg[:, None, None]), 0.0)
    scores = jnp.einsum('bihd,bjhd->bhij', qf, kf) * decay[None]
    o = jnp.einsum('bhij,bjhe->bihe', scores, vf)
    o = o + jnp.einsum('bihd,bhde->bihe', qf, S) * jnp.exp((idx + 1.0)[:, None] * lg[None, :])[None, :, :, None]
    k_dec = kf * jnp.exp((L - 1.0 - idx)[:, None] * lg[None, :])[None, :, :, None]
    S_new = jnp.exp(L * lg)[None, :, None, None] * S + jnp.einsum('bjhd,bjhe->bhde', k_dec, vf)
    return o, S_new


def retention_prompt(q, k, v):
    B, S = q.shape[:2]
    n = S // RET_CHUNK

    def to_chunks(t):
        return jnp.moveaxis(t.reshape(B, n, RET_CHUNK, *t.shape[2:]), 1, 0)

    def step(Sc, qkv):
        o, Sn = retention_chunk(Sc, *qkv)
        return Sn, o

    S0 = jnp.zeros((B, H_RET, DK_RET, DV_RET), jnp.float32)
    S_fin, o = lax.scan(step, S0, (to_chunks(q), to_chunks(k), to_chunks(v)))
    return jnp.moveaxis(o, 0, 1).reshape(B, S, H_RET, DV_RET), S_fin


def banded_attention(q, k, v, steps):
    N, n, H, D = q.shape
    nb = n // ATT_BLOCK
    qb = q.reshape(N, nb, ATT_BLOCK, H, D)

    def with_prev(t):
        t = t.reshape(N, nb, ATT_BLOCK, H, D)
        prev = jnp.concatenate([jnp.zeros_like(t[:, :1]), t[:, :-1]], axis=1)
        return jnp.concatenate([prev, t], axis=2)

    kk, vv = with_prev(k), with_prev(v)
    s = jnp.einsum('bcqhd,bckhd->bchqk', qb, kk) * (D ** -0.5)
    qi = jnp.arange(ATT_BLOCK)[:, None] + ATT_BLOCK
    ki = jnp.arange(2 * ATT_BLOCK)[None, :]
    dist = qi - ki
    band = (dist >= 0) & (dist <= steps)
    not_before_start = (jnp.arange(nb)[:, None, None] > 0) | (ki >= ATT_BLOCK)[None]
    valid = band[None] & not_before_start
    s = jnp.where(valid[None, :, None], s, -jnp.inf)
    m = jnp.max(s, axis=-1, keepdims=True)
    p = jnp.exp(s - m)
    l = jnp.sum(p, axis=-1)
    o = jnp.einsum('bchqk,bckhd->bcqhd', p, vv) / jnp.swapaxes(l, 2, 3)[..., None]
    m = jnp.swapaxes(m[..., 0], 2, 3).reshape(N, n, H)
    l = jnp.swapaxes(l, 2, 3).reshape(N, n, H)
    return o.reshape(N, n, H, D), m, l


def combine_dilations(parts):
    os_ = jnp.stack([p[0] for p in parts])
    ms = jnp.stack([p[1] for p in parts])
    ls = jnp.stack([p[2] for p in parts])
    M = jnp.max(ms, axis=0, keepdims=True)
    w = ls * jnp.exp(ms - M)
    return jnp.sum(w[..., None] * os_, axis=0) / jnp.sum(w, axis=0)[..., None]


def dilated_prompt(q, k, v):
    B, S, H, D = q.shape
    q, k, v = q.astype(jnp.float32), k.astype(jnp.float32), v.astype(jnp.float32)
    parts = []
    for window, dil in DILATED_PAIRS:
        span = dil * ATT_BLOCK
        S_pad = -(-S // span) * span
        n = S_pad // dil

        def by_residue(t):
            t = jnp.pad(t, ((0, 0), (0, S_pad - S), (0, 0), (0, 0))).reshape(B, n, dil, H, D)
            return jnp.swapaxes(t, 1, 2).reshape(B * dil, n, H, D)

        def back(t):
            rest = t.shape[2:]
            t = jnp.swapaxes(t.reshape(B, dil, n, *rest), 1, 2)
            return t.reshape(B, S_pad, *rest)[:, :S]

        o, m, l = banded_attention(by_residue(q), by_residue(k), by_residue(v), window // dil)
        parts.append((back(o), back(m), back(l)))
    return combine_dilations(parts)


def dilated_sample(q, k, v, buf_k, buf_v):
    WB = buf_k.shape[1]
    L, D = q.shape[1], q.shape[-1]
    kk = jnp.concatenate([buf_k, k.astype(buf_k.dtype)], axis=1)
    vv = jnp.concatenate([buf_v, v.astype(buf_v.dtype)], axis=1)
    kf, vf, qf = kk.astype(jnp.float32), vv.astype(jnp.float32), q.astype(jnp.float32)
    q_idx = WB + jnp.arange(L)
    parts = []
    for window, dil in DILATED_PAIRS:
        steps = window // dil
        idx = q_idx[:, None] - dil * jnp.arange(steps + 1)[None, :]
        valid = idx >= 0
        safe = jnp.maximum(idx, 0)
        kg = kf[:, safe]
        vg = vf[:, safe]
        s = jnp.einsum('blhd,bljhd->blhj', qf, kg) * (D ** -0.5)
        s = jnp.where(valid[None, :, None, :], s, -jnp.inf)
        m = jnp.max(s, axis=-1, keepdims=True)
        p = jnp.exp(s - m)
        l = jnp.sum(p, axis=-1)
        o = jnp.einsum('blhj,bljhd->blhd', p, vg) / l[..., None]
        parts.append((o, m[..., 0], l))
    return combine_dilations(parts), kk[:, -WB:], vv[:, -WB:]


def head_rmsnorm(o, gain):
    B, L = o.shape[:2]
    o = o * lax.rsqrt(jnp.mean(o * o, axis=-1, keepdims=True) + EPS)
    return o.reshape(B, L, -1) * gain.astype(jnp.float32)


def merge_groups(o_r, g_r, o_a, beta_ret, beta_att, w_out):
    r = head_rmsnorm(o_r, beta_ret) * jax.nn.silu(g_r.astype(jnp.float32))
    a = head_rmsnorm(o_a, beta_att)
    return jnp.concatenate([r, a], axis=-1).astype(w_out.dtype) @ w_out


def moe(h, router_w, router_b, w_up, b_up, w_down, b_down):
    B, L, D = h.shape
    t = h.reshape(-1, D)
    T = t.shape[0]
    logits = t.astype(jnp.float32) @ router_w.astype(jnp.float32) + router_b.astype(jnp.float32)
    top_v, top_i = lax.top_k(logits, TOP_K)
    gates = jax.nn.softmax(top_v, axis=-1)
    A = T * TOP_K
    flat_e = top_i.reshape(-1)
    flat_tok = jnp.arange(A, dtype=jnp.int32) // TOP_K
    order = jnp.argsort(flat_e)
    se = flat_e[order]
    counts = jnp.bincount(flat_e, length=N_EXPERTS)
    padded = (counts + MOE_BLOCK - 1) // MOE_BLOCK * MOE_BLOCK
    pad_end = jnp.cumsum(padded)
    pad_start = pad_end - padded
    start = jnp.cumsum(counts) - counts
    dest = pad_start[se] + jnp.arange(A) - start[se]
    n_blocks = -(-(A + N_EXPERTS * (MOE_BLOCK - 1)) // MOE_BLOCK)
    R = n_blocks * MOE_BLOCK
    row_tok = jnp.full((R,), T, jnp.int32).at[dest].set(flat_tok[order])
    row_gate = jnp.zeros((R,), jnp.float32).at[dest].set(gates.reshape(-1)[order])
    blk_exp = jnp.minimum(jnp.searchsorted(pad_end, jnp.arange(n_blocks) * MOE_BLOCK, side='right'), N_EXPERTS - 1)
    t_pad = jnp.concatenate([t, jnp.zeros((1, D), t.dtype)], axis=0)
    xb = t_pad[row_tok].reshape(n_blocks, MOE_BLOCK, D)

    def expert_block(args):
        xe, e = args
        u = xe @ w_up[e] + b_up[e]
        x_glu = jnp.minimum(u[:, :D_FF], SWIGLU_LIMIT)
        x_lin = jnp.clip(u[:, D_FF:], -SWIGLU_LIMIT, SWIGLU_LIMIT)
        act = x_glu * jax.nn.sigmoid(SWIGLU_ALPHA * x_glu) * (x_lin + 1.0)
        return act @ w_down[e] + b_down[e]

    yb = lax.map(expert_block, (xb, blk_exp)).reshape(R, D)
    y = jnp.zeros((T + 1, D), jnp.float32).at[row_tok].add(yb.astype(jnp.float32) * row_gate[:, None])
    return y[:T].reshape(B, L, D).astype(h.dtype)


def trunk_layer(x, c, pos, lw, mixers):
    (ada_w, ada_b, n1, w_in, beta_ret, beta_att, w_out, n2, rw, rb, wu, bu, wd, bd) = lw
    sh1, sc1, g1, sh2, sc2, g2 = adaln(c, ada_w, ada_b, N_MOD)
    h = modulate(rmsnorm(x, n1), sh1, sc1)
    qr, kr, vr, gr, qa, ka, va = project(h, w_in, pos)
    o_r, o_a, new_state = mixers(qr, kr, vr, qa, ka, va)
    x = (x + g1 * merge_groups(o_r, gr, o_a, beta_ret, beta_att, w_out)).astype(x.dtype)
    h2 = modulate(rmsnorm(x, n2), sh2, sc2)
    x = (x + g2 * moe(h2, rw, rb, wu, bu, wd, bd)).astype(x.dtype)
    return x, new_state


def prompt_mixers(qr, kr, vr, qa, ka, va):
    o_r, S_fin = retention_prompt(qr, kr, vr)
    o_a = dilated_prompt(qa, ka, va)
    keep = min(MAX_WINDOW, qa.shape[1])
    return o_r, o_a, (S_fin.astype(qr.dtype), ka[:, -keep:], va[:, -keep:])


def sample_mixers(state, buf_k, buf_v, qr, kr, vr, qa, ka, va):
    o_r, S_new = retention_chunk(state.astype(jnp.float32), qr, kr, vr)
    o_a, new_k, new_v = dilated_sample(qa, ka, va, buf_k, buf_v)
    return o_r, o_a, (S_new.astype(state.dtype), new_k, new_v)


def setup_inputs(seed: int = 0) -> dict:
    key = jax.random.key(seed)
    ks = jax.random.split(key, 26)
    WB = min(MAX_WINDOW, PAST_LEN)

    def nrm(k, shape, scale):
        return jax.random.normal(k, shape, jnp.float32) * scale

    def gain(k, shape):
        return 1.0 + nrm(k, shape, 0.05)

    return {
        'x_prompt': nrm(ks[0], (BATCH, SEQ, D_MODEL), 1.0),
        'x_sample': nrm(ks[1], (DEC_BATCH, DEC_SEQ, D_MODEL), 1.0),
        'state_ret': nrm(ks[2], (DEPTH, DEC_BATCH, H_RET, DK_RET, DV_RET), 1.0),
        'cache_win_k': nrm(ks[3], (DEPTH, DEC_BATCH, WB, H_ATT, HD_ATT), 1.0),
        'cache_win_v': nrm(ks[4], (DEPTH, DEC_BATCH, WB, H_ATT, HD_ATT), 1.0),
        'c_prompt': nrm(ks[5], (BATCH, D_MODEL), 1.0),
        'c_sample': nrm(ks[6], (DEC_BATCH, D_MODEL), 1.0),
        'ada_w': nrm(ks[7], (DEPTH, D_MODEL, N_MOD * D_MODEL), 0.2 * D_MODEL ** -0.5),
        'ada_b': nrm(ks[8], (DEPTH, N_MOD * D_MODEL), 0.02),
        'norm1_g': gain(ks[9], (DEPTH, D_MODEL)),
        'w_in': nrm(ks[10], (DEPTH, D_MODEL, IN_WIDTH), D_MODEL ** -0.5),
        'beta_ret': gain(ks[11], (DEPTH, RET_V)),
        'beta_att': gain(ks[12], (DEPTH, ATT_W)),
        'w_out': nrm(ks[13], (DEPTH, MIX_WIDTH, D_MODEL), MIX_WIDTH ** -0.5),
        'norm2_g': gain(ks[14], (DEPTH, D_MODEL)),
        'router_w': nrm(ks[15], (DEPTH, D_MODEL, N_EXPERTS), D_MODEL ** -0.5),
        'router_b': nrm(ks[16], (DEPTH, N_EXPERTS), 0.01),
        'w_up': nrm(ks[17], (DEPTH, N_EXPERTS, D_MODEL, 2 * D_FF), D_MODEL ** -0.5),
        'b_up': nrm(ks[18], (DEPTH, N_EXPERTS, 2 * D_FF), 0.01),
        'w_down': nrm(ks[19], (DEPTH, N_EXPERTS, D_FF, D_MODEL), D_FF ** -0.5),
        'b_down': nrm(ks[20], (DEPTH, N_EXPERTS, D_MODEL), 0.01),
        'final_ada_w': nrm(ks[21], (D_MODEL, 2 * D_MODEL), 0.2 * D_MODEL ** -0.5),
        'final_ada_b': nrm(ks[22], (2 * D_MODEL,), 0.02),
        'final_norm_g': gain(ks[23], (D_MODEL,)),
    }


def reference(x_prompt, x_sample, state_ret, cache_win_k, cache_win_v, c_prompt, c_sample,
              ada_w, ada_b, norm1_g, w_in, beta_ret, beta_att, w_out, norm2_g,
              router_w, router_b, w_up, b_up, w_down, b_down,
              final_ada_w, final_ada_b, final_norm_g):
    pos_p = jnp.arange(x_prompt.shape[1])
    pos_s = PAST_LEN + jnp.arange(x_sample.shape[1])
    xp, xs = x_prompt, x_sample
    rs_p, rs_s, wk_p, wv_p, wk_s, wv_s = [], [], [], [], [], []
    for l in range(DEPTH):
        lw = (ada_w[l], ada_b[l], norm1_g[l], w_in[l], beta_ret[l], beta_att[l], w_out[l],
              norm2_g[l], router_w[l], router_b[l], w_up[l], b_up[l], w_down[l], b_down[l])
        xp, (sp, kp, vp) = trunk_layer(xp, c_prompt, pos_p, lw, prompt_mixers)
        mix_s = functools.partial(sample_mixers, state_ret[l], cache_win_k[l], cache_win_v[l])
        xs, (ss, ks_, vs_) = trunk_layer(xs, c_sample, pos_s, lw, mix_s)
        rs_p.append(sp); wk_p.append(kp); wv_p.append(vp)
        rs_s.append(ss); wk_s.append(ks_); wv_s.append(vs_)
    shp, scp = adaln(c_prompt, final_ada_w, final_ada_b, 2)
    shs, scs = adaln(c_sample, final_ada_w, final_ada_b, 2)
    y_prompt = modulate(rmsnorm(xp, final_norm_g), shp, scp)
    y_sample = modulate(rmsnorm(xs, final_norm_g), shs, scs)
    return (y_prompt, y_sample, jnp.stack(rs_p), jnp.stack(rs_s),
            jnp.stack(wk_p), jnp.stack(wv_p), jnp.stack(wk_s), jnp.stack(wv_s))
```

```python
import functools
import math

import numpy as np
import jax
import jax.numpy as jnp
from jax import lax
from jax.experimental import pallas as pl
from jax.experimental.pallas import tpu as pltpu

F32 = jnp.float32
BF16 = jnp.bfloat16
HIGHEST = lax.Precision.HIGHEST

D_MODEL = 1024
PAST_LEN = 16384
H_RET, DK_RET, DV_RET = 4, 64, 128
RET_CHUNK = 128
H_ATT, HD_ATT = 8, 64
DILATED_PAIRS = ((128, 1), (512, 4), (2048, 16))
ATT_BLOCK = 128
MAX_WINDOW = 2048
ROPE_THETA = 10000.0
RET_QK = H_RET * DK_RET
RET_V = H_RET * DV_RET
ATT_W = H_ATT * HD_ATT
IN_WIDTH = 2 * RET_QK + 2 * RET_V + 3 * ATT_W
N_EXPERTS = 32
TOP_K = 4
D_FF = 1024
SWIGLU_LIMIT = 7.0
SWIGLU_ALPHA = 1.702
N_MOD = 6
EPS = 1e-6

LANES = 128
VMEM_LIMIT = 56 * 1024 * 1024
ROW_TILE = 512
MOE_TILE = 512
COMBINE_TILE = 256

RET_LOG_DECAY = tuple(math.log(1.0 - 2.0 ** (-5.0 - h)) for h in range(H_RET))


def _params(sem, vmem=VMEM_LIMIT):
    return pltpu.CompilerParams(dimension_semantics=sem, vmem_limit_bytes=vmem)


def _sigmoid(z):
    return 1.0 / (1.0 + jnp.exp(-z))


def _adaln_body(c_ref, w_ref, b_ref, o_ref):
    c = c_ref[...]
    s = c * _sigmoid(c)
    o_ref[...] = jnp.dot(s, w_ref[...], precision=HIGHEST, preferred_element_type=F32) + b_ref[...]


def _adaln(c, w, b):
    R, D = c.shape
    N = w.shape[1]
    tn = 1024
    return pl.pallas_call(
        _adaln_body,
        grid=(N // tn,),
        in_specs=[pl.BlockSpec((R, D), lambda j: (0, 0)),
                  pl.BlockSpec((D, tn), lambda j: (0, j)),
                  pl.BlockSpec((1, tn), lambda j: (0, j))],
        out_specs=pl.BlockSpec((R, tn), lambda j: (0, j)),
        out_shape=jax.ShapeDtypeStruct((R, N), F32),
        compiler_params=_params(("arbitrary",)),
        name="adaln",
    )(c, w, b.reshape(1, N))


def _rope_chunk(xc, cos, sin_signed, first_half):
    partner = jnp.where(first_half, pltpu.roll(xc, 96, 1), pltpu.roll(xc, 32, 1))
    return xc * cos + partner * sin_signed


def _inproj_body(x_ref, sh_ref, sc_ref, g_ref, w_ref, cos_ref, sin_ref,
                 qr_ref, kr_ref, vr_ref, gr_ref, qa_ref, ka_ref, va_ref):
    x = x_ref[...]
    ms = jnp.mean(x * x, axis=-1, keepdims=True)
    h = x * lax.rsqrt(ms + EPS) * g_ref[...]
    h = h * (1.0 + sc_ref[...]) + sh_ref[...]
    p = jnp.dot(h.astype(BF16), w_ref[...], preferred_element_type=F32)
    cos = cos_ref[...]
    sin = sin_ref[...]
    first_half = (lax.broadcasted_iota(jnp.int32, (1, LANES), 1) % HD_ATT) < (HD_ATT // 2)

    def rope_to(ref, off, width, scale):
        for c in range(width // LANES):
            xc = p[:, off + c * LANES: off + (c + 1) * LANES]
            r = _rope_chunk(xc, cos, sin, first_half)
            if scale != 1.0:
                r = r * scale
            ref[:, c * LANES:(c + 1) * LANES] = r

    o = 0
    rope_to(qr_ref, o, RET_QK, 1.0); o += RET_QK
    rope_to(kr_ref, o, RET_QK, DK_RET ** -0.5); o += RET_QK
    vr_ref[...] = p[:, o:o + RET_V]; o += RET_V
    gr_ref[...] = p[:, o:o + RET_V]; o += RET_V
    rope_to(qa_ref, o, ATT_W, 1.0); o += ATT_W
    rope_to(ka_ref, o, ATT_W, 1.0); o += ATT_W
    va_ref[...] = p[:, o:o + ATT_W]


def _mod_spec(mod, tm, tiles_per_group):
    R = mod.shape[1]
    return pl.BlockSpec((None, R, mod.shape[2]), lambda i: (i // tiles_per_group, 0, 0))


def _inproj(x2d, sh, sc, g, w_bf, cos_t, sin_t, tm, tiles_per_group):
    T, D = x2d.shape
    pos_tiles = cos_t.shape[0] // tm
    widths = (RET_QK, RET_QK, RET_V, RET_V, ATT_W, ATT_W, ATT_W)
    row = lambda w: pl.BlockSpec((tm, w), lambda i: (i, 0))
    tab = pl.BlockSpec((tm, LANES), lambda i: (i % pos_tiles, 0))
    return pl.pallas_call(
        _inproj_body,
        grid=(T // tm,),
        in_specs=[row(D), _mod_spec(sh, tm, tiles_per_group), _mod_spec(sc, tm, tiles_per_group),
                  pl.BlockSpec((1, D), lambda i: (0, 0)),
                  pl.BlockSpec((D, IN_WIDTH), lambda i: (0, 0)), tab, tab],
        out_specs=[row(w) for w in widths],
        out_shape=[jax.ShapeDtypeStruct((T, w), F32) for w in widths],
        compiler_params=_params(("arbitrary",)),
        name="inproj",
    )(x2d, sh, sc, g.reshape(1, D), w_bf, cos_t, sin_t)


def _rope_tables(pos):
    half = HD_ATT // 2
    inv = ROPE_THETA ** (-jnp.arange(half, dtype=F32) / half)
    ang = pos.astype(F32)[:, None] * inv[None, :]
    cos = jnp.cos(ang)
    sin = jnp.sin(ang)
    cos_t = jnp.tile(cos, (1, LANES // half))
    sin_t = jnp.tile(jnp.concatenate([-sin, sin], axis=1), (1, LANES // HD_ATT))
    return cos_t, sin_t


def _retention_body(q_ref, k_ref, v_ref, s0_ref, o_ref, sfin_ref, s_scr, *, L):
    c = pl.program_id(1)

    @pl.when(c == 0)
    def _():
        s_scr[...] = s0_ref[...]

    ii = lax.broadcasted_iota(jnp.int32, (L, L), 0)
    jj = lax.broadcasted_iota(jnp.int32, (L, L), 1)
    diff = (ii - jj).astype(F32)
    causal = diff >= 0
    row = lax.broadcasted_iota(jnp.int32, (L, 1), 0).astype(F32)
    for h in range(H_RET):
        lg = RET_LOG_DECAY[h]
        decay = jnp.where(causal, jnp.exp(jnp.where(causal, diff, 0.0) * lg), 0.0)
        q = q_ref[:, h * DK_RET:(h + 1) * DK_RET]
        k = k_ref[:, h * DK_RET:(h + 1) * DK_RET]
        v = v_ref[:, h * DV_RET:(h + 1) * DV_RET]
        qb, vb = q.astype(BF16), v.astype(BF16)
        scores = lax.dot_general(qb, k.astype(BF16), (((1,), (1,)), ((), ())),
                                 preferred_element_type=F32) * decay
        S = s_scr[h]
        o = jnp.dot(scores.astype(BF16), vb, preferred_element_type=F32)
        o = o + jnp.dot(qb, S.astype(BF16), preferred_element_type=F32) * jnp.exp((row + 1.0) * lg)
        k_dec = k * jnp.exp((L - 1.0 - row) * lg)
        s_scr[h] = math.exp(L * lg) * S + lax.dot_general(
            k_dec.astype(BF16), vb, (((0,), (0,)), ((), ())), preferred_element_type=F32)
        o_ref[:, h * DV_RET:(h + 1) * DV_RET] = o

    @pl.when(c == pl.num_programs(1) - 1)
    def _():
        sfin_ref[...] = s_scr[...]


def _retention(q2d, k2d, v2d, s0, B, L_seq, L):
    nc = L_seq // L
    row = lambda w: pl.BlockSpec((L, w), lambda b, c: (b * nc + c, 0))
    st = pl.BlockSpec((None, H_RET, DK_RET, DV_RET), lambda b, c: (b, 0, 0, 0))
    return pl.pallas_call(
        functools.partial(_retention_body, L=L),
        grid=(B, nc),
        in_specs=[row(RET_QK), row(RET_QK), row(RET_V), st],
        out_specs=[row(RET_V), st],
        out_shape=[jax.ShapeDtypeStruct((B * L_seq, RET_V), F32),
                   jax.ShapeDtypeStruct((B, H_RET, DK_RET, DV_RET), F32)],
        scratch_shapes=[pltpu.VMEM((H_RET, DK_RET, DV_RET), F32)],
        compiler_params=_params(("arbitrary", "arbitrary")),
        name="retention",
    )(q2d, k2d, v2d, s0)


def _dil_body(*refs, first, last):
    if first:
        q_ref, kc_ref, kp_ref, vc_ref, vp_ref, acc_o, m_o, l_o = refs
    elif last:
        q_ref, kc_ref, kp_ref, vc_ref, vp_ref, acc_i, m_i, l_i, o_ref = refs
    else:
        q_ref, kc_ref, kp_ref, vc_ref, vp_ref, acc_i, m_i, l_i, acc_o, m_o, l_o = refs
    blk = pl.program_id(1)
    nq = ATT_BLOCK
    qi = lax.broadcasted_iota(jnp.int32, (nq, 2 * nq), 0) + nq
    ki = lax.broadcasted_iota(jnp.int32, (nq, 2 * nq), 1)
    dist = qi - ki
    valid = (dist >= 0) & (dist <= nq) & ((blk > 0) | (ki >= nq))
    lane = lax.broadcasted_iota(jnp.int32, (1, LANES), 1)
    low = lane < HD_ATT
    if not last:
        m_o[...] = jnp.zeros((nq, LANES), F32)
        l_o[...] = jnp.zeros((nq, LANES), F32)
    for hp in range(H_ATT // 2):
        cs = slice(hp * LANES, (hp + 1) * LANES)
        q = q_ref[:, cs]
        kcat = jnp.concatenate([kp_ref[:, cs], kc_ref[:, cs]], axis=0).astype(BF16)
        vcat = jnp.concatenate([vp_ref[:, cs], vc_ref[:, cs]], axis=0).astype(BF16)
        halves = []
        for hh in range(2):
            h = 2 * hp + hh
            qh = jnp.where(low if hh == 0 else ~low, q, 0.0).astype(BF16)
            s = lax.dot_general(qh, kcat, (((1,), (1,)), ((), ())),
                                preferred_element_type=F32) * (HD_ATT ** -0.5)
            s = jnp.where(valid, s, -jnp.inf)
            m_d = jnp.max(s, axis=-1, keepdims=True)
            if first:
                m_new = m_d
                p = jnp.exp(s - m_new)
                l_new = jnp.sum(p, axis=-1, keepdims=True)
                acc = jnp.dot(p.astype(BF16), vcat, preferred_element_type=F32)
            else:
                m_old = m_i[:, h:h + 1]
                l_old = l_i[:, h:h + 1]
                m_new = jnp.maximum(m_old, m_d)
                alpha = jnp.exp(m_old - m_new)
                p = jnp.exp(s - m_new)
                l_new = alpha * l_old + jnp.sum(p, axis=-1, keepdims=True)
                acc = alpha * acc_i[:, cs] + jnp.dot(p.astype(BF16), vcat, preferred_element_type=F32)
            if last:
                acc = acc / l_new
            else:
                m_o[:, h:h + 1] = m_new
                l_o[:, h:h + 1] = l_new
            halves.append(acc)
        res = jnp.where(low, halves[0], halves[1])
        if last:
            o_ref[:, cs] = res
        else:
            acc_o[:, cs] = res


def _dilated_step(q3, k3, v3, state, dil, first, last):
    B, S, W = q3.shape
    n = S // dil
    nb = n // ATT_BLOCK
    view = lambda t, w: t.reshape(B, n, dil * w)
    cur = lambda w: pl.BlockSpec((None, ATT_BLOCK, w), lambda b, i, r: (b, i, r))
    prev = lambda w: pl.BlockSpec((None, ATT_BLOCK, w), lambda b, i, r: (b, jnp.maximum(i - 1, 0), r))
    ins = [view(q3, W), view(k3, W), view(k3, W), view(v3, W), view(v3, W)]
    in_specs = [cur(W), cur(W), prev(W), cur(W), prev(W)]
    if not first:
        acc, m, l = state
        ins += [view(acc, W), view(m, LANES), view(l, LANES)]
        in_specs += [cur(W), cur(LANES), cur(LANES)]
    if last:
        out_specs = cur(W)
        out_shape = jax.ShapeDtypeStruct((B, n, dil * W), F32)
    else:
        out_specs = [cur(W), cur(LANES), cur(LANES)]
        out_shape = [jax.ShapeDtypeStruct((B, n, dil * W), F32),
                     jax.ShapeDtypeStruct((B, n, dil * LANES), F32),
                     jax.ShapeDtypeStruct((B, n, dil * LANES), F32)]
    out = pl.pallas_call(
        functools.partial(_dil_body, first=first, last=last),
        grid=(B, nb, dil),
        in_specs=in_specs, out_specs=out_specs, out_shape=out_shape,
        compiler_params=_params(("arbitrary", "arbitrary", "arbitrary")),
        name=f"dilated_{dil}",
    )(*ins)
    if last:
        return out.reshape(B, S, W)
    acc, m, l = out
    return acc.reshape(B, S, W), m.reshape(B, S, LANES), l.reshape(B, S, LANES)


def _dilated_prompt(q3, k3, v3):
    state = None
    n_d = len(DILATED_PAIRS)
    for j, (window, dil) in enumerate(DILATED_PAIRS):
        assert window // dil == ATT_BLOCK and q3.shape[1] % (dil * ATT_BLOCK) == 0
        state = _dilated_step(q3, k3, v3, state, dil, j == 0, j == n_d - 1)
    return state


def _sample_multiplicity(L, WB):
    pad = LANES
    cnt = np.zeros((L, WB + pad), np.float32)
    for l in range(L):
        for window, dil in DILATED_PAIRS:
            for j in range(window // dil + 1):
                idx = WB + l - dil * j
                if idx >= 0:
                    cnt[l, idx] += 1.0
    cnt = np.tile(cnt, (H_ATT, 1))
    return cnt[:, :WB], cnt[:, WB:]


def _sample_att_body(q_ref, kn_ref, vn_ref, bk_ref, bv_ref, cb_ref, cn_ref,
                     o_ref, nk_ref, nv_ref, *, L, WB):
    q = q_ref[...]
    lane_head = lax.broadcasted_iota(jnp.int32, (1, ATT_W), 1) // HD_ATT
    qexp = jnp.concatenate([jnp.where(lane_head == h, q, 0.0) for h in range(H_ATT)], axis=0).astype(BF16)
    zpad = jnp.zeros((LANES - L, ATT_W), F32)
    kn = jnp.concatenate([kn_ref[...], zpad], axis=0).astype(BF16)
    vn = jnp.concatenate([vn_ref[...], zpad], axis=0).astype(BF16)
    bk = bk_ref[...]
    bv = bv_ref[...]
    nt = (((1,), (1,)), ((), ()))
    scale = HD_ATT ** -0.5
    sb = lax.dot_general(qexp, bk.astype(BF16), nt, preferred_element_type=F32) * scale
    sn = lax.dot_general(qexp, kn, nt, preferred_element_type=F32) * scale
    cb = cb_ref[...]
    cn = cn_ref[...]
    sb = jnp.where(cb > 0, sb, -jnp.inf)
    sn = jnp.where(cn > 0, sn, -jnp.inf)
    m = jnp.maximum(jnp.max(sb, axis=-1, keepdims=True), jnp.max(sn, axis=-1, keepdims=True))
    pb = cb * jnp.exp(sb - m)
    pn = cn * jnp.exp(sn - m)
    l = jnp.sum(pb, axis=-1, keepdims=True) + jnp.sum(pn, axis=-1, keepdims=True)
    o_all = (jnp.dot(pb.astype(BF16), bv.astype(BF16), preferred_element_type=F32)
             + jnp.dot(pn.astype(BF16), vn, preferred_element_type=F32)) / l
    o = jnp.zeros((L, ATT_W), F32)
    for h in range(H_ATT):
        o = o + jnp.where(lane_head == h, o_all[h * L:(h + 1) * L], 0.0)
    o_ref[...] = o
    nk_ref[0:WB - L, :] = bk[L:WB]
    nk_ref[WB - L:WB, :] = kn_ref[...]
    nv_ref[0:WB - L, :] = bv[L:WB]
    nv_ref[WB - L:WB, :] = vn_ref[...]


def _sample_attention(q2d, k2d, v2d, buf_k, buf_v, DB, L):
    WB = buf_k.shape[1]
    cb, cn = _sample_multiplicity(L, WB)
    row = pl.BlockSpec((L, ATT_W), lambda b: (b, 0))
    buf = pl.BlockSpec((None, WB, ATT_W), lambda b: (b, 0, 0))
    const = lambda a: pl.BlockSpec(a.shape, lambda b: (0, 0))
    return pl.pallas_call(
        functools.partial(_sample_att_body, L=L, WB=WB),
        grid=(DB,),
        in_specs=[row, row, row, buf, buf, const(cb), const(cn)],
        out_specs=[row, buf, buf],
        out_shape=[jax.ShapeDtypeStruct((DB * L, ATT_W), F32),
                   jax.ShapeDtypeStruct((DB, WB, ATT_W), F32),
                   jax.ShapeDtypeStruct((DB, WB, ATT_W), F32)],
        compiler_params=_params(("arbitrary",)),
        name="sample_attention",
    )(q2d, k2d, v2d, buf_k, buf_v, jnp.asarray(cb), jnp.asarray(cn))


def _group_mean(sq, width):
    if width == LANES:
        return jnp.mean(sq, axis=-1, keepdims=True)
    lane = lax.broadcasted_iota(jnp.int32, (1, LANES), 1)
    low = lane < width
    s_lo = jnp.sum(jnp.where(low, sq, 0.0), axis=-1, keepdims=True)
    s_hi = jnp.sum(jnp.where(low, 0.0, sq), axis=-1, keepdims=True)
    return jnp.where(low, s_lo, s_hi) * (1.0 / width)


def _merge_body(or_ref, gr_ref, oa_ref, x_ref, g1_ref, sh_ref, sc_ref, br_ref, ba_ref, wo_ref,
                n2_ref, rw_ref, rb_ref, x1_ref, h2_ref, ti_ref, gt_ref):
    parts = []
    for c in range(RET_V // LANES):
        cs = slice(c * LANES, (c + 1) * LANES)
        o = or_ref[:, cs]
        o = o * lax.rsqrt(_group_mean(o * o, DV_RET) + EPS) * br_ref[:, cs]
        g = gr_ref[:, cs]
        parts.append((o * (g * _sigmoid(g))).astype(BF16))
    for c in range(ATT_W // LANES):
        cs = slice(c * LANES, (c + 1) * LANES)
        o = oa_ref[:, cs]
        o = o * lax.rsqrt(_group_mean(o * o, HD_ATT) + EPS) * ba_ref[:, cs]
        parts.append(o.astype(BF16))
    mix = jnp.concatenate(parts, axis=1)
    y = jnp.dot(mix, wo_ref[...], preferred_element_type=F32)
    x1 = x_ref[...] + g1_ref[...] * y
    x1_ref[...] = x1
    ms = jnp.mean(x1 * x1, axis=-1, keepdims=True)
    h2 = x1 * lax.rsqrt(ms + EPS) * n2_ref[...]
    h2 = h2 * (1.0 + sc_ref[...]) + sh_ref[...]
    h2_ref[...] = h2
    logits = jnp.dot(h2, rw_ref[...], precision=HIGHEST, preferred_element_type=F32) + rb_ref[...]
    lane = lax.broadcasted_iota(jnp.int32, logits.shape, 1).astype(F32)
    vals, idxs = [], []
    for _ in range(TOP_K):
        m = jnp.max(logits, axis=-1, keepdims=True)
        idx = jnp.min(jnp.where(logits == m, lane, float(N_EXPERTS)), axis=-1, keepdims=True)
        vals.append(m)
        idxs.append(idx)
        logits = jnp.where(lane == idx, -jnp.inf, logits)
    e = [jnp.exp(v - vals[0]) for v in vals]
    tot = e[0] + e[1] + e[2] + e[3]
    for k in range(TOP_K):
        ti_ref[:, k:k + 1] = idxs[k].astype(jnp.int32)
        gt_ref[:, k:k + 1] = e[k] / tot


def _merge(o_r, g_r, o_a, x2d, g1, sh2, sc2, beta_ret, beta_att, wo_bf, n2, rw, rb, tm, tiles_per_group):
    T, D = x2d.shape
    row = lambda w: pl.BlockSpec((tm, w), lambda i: (i, 0))
    const = lambda r, c: pl.BlockSpec((r, c), lambda i: (0, 0))
    ms = lambda m: _mod_spec(m, tm, tiles_per_group)
    return pl.pallas_call(
        _merge_body,
        grid=(T // tm,),
        in_specs=[row(RET_V), row(RET_V), row(ATT_W), row(D), ms(g1), ms(sh2), ms(sc2),
                  const(1, RET_V), const(1, ATT_W), const(D, D), const(1, D),
                  const(D, N_EXPERTS), const(1, N_EXPERTS)],
        out_specs=[row(D), row(D), row(TOP_K), row(TOP_K)],
        out_shape=[jax.ShapeDtypeStruct((T, D), F32), jax.ShapeDtypeStruct((T, D), F32),
                   jax.ShapeDtypeStruct((T, TOP_K), jnp.int32), jax.ShapeDtypeStruct((T, TOP_K), F32)],
        compiler_params=_params(("arbitrary",)),
        name="merge_router",
    )(o_r, g_r, o_a, x2d, g1, sh2, sc2, beta_ret.reshape(1, RET_V), beta_att.reshape(1, ATT_W),
      wo_bf, n2.reshape(1, D), rw, rb.reshape(1, N_EXPERTS))


def _moe_body(nused_ref, bexp_ref, rt_ref, rtn_ref, h_hbm, wu_ref, bu_ref, wd_ref, bd_ref,
              y_ref, xbuf, sem, *, tm):
    i = pl.program_id(0)
    n_used = nused_ref[0]

    def gather(tok_ref, slot):
        def issue(r, carry):
            pltpu.make_async_copy(h_hbm.at[pl.ds(tok_ref[r], 1)], xbuf.at[slot, pl.ds(r, 1)],
                                  sem.at[slot]).start()
            return carry
        lax.fori_loop(0, tm, issue, 0, unroll=8)

    @pl.when(i == 0)
    def _():
        gather(rt_ref, 0)

    @pl.when(i + 1 < n_used)
    def _():
        gather(rtn_ref, (i + 1) % 2)

    @pl.when(i < n_used)
    def _():
        slot = i % 2
        pltpu.make_async_copy(h_hbm.at[pl.ds(0, tm)], xbuf.at[slot], sem.at[slot]).wait()
        x = xbuf[slot].astype(BF16)
        u = jnp.dot(x, wu_ref[...], preferred_element_type=F32) + bu_ref[...]
        x_glu = jnp.minimum(u[:, :D_FF], SWIGLU_LIMIT)
        x_lin = jnp.clip(u[:, D_FF:], -SWIGLU_LIMIT, SWIGLU_LIMIT)
        act = x_glu * _sigmoid(SWIGLU_ALPHA * x_glu) * (x_lin + 1.0)
        y_ref[...] = jnp.dot(act.astype(BF16), wd_ref[...], preferred_element_type=F32) + bd_ref[...]

    @pl.when(i >= n_used)
    def _():
        y_ref[...] = jnp.zeros(y_ref.shape, F32)


def _moe_experts(h2, row_tok, blk_exp, n_used, wu_bf, bu, wd_bf, bd, tm):
    T, D = h2.shape
    R = row_tok.shape[0]
    nb = R // tm
    smem = lambda f: pl.BlockSpec((tm,), f, memory_space=pltpu.SMEM)
    grid_spec = pltpu.PrefetchScalarGridSpec(
        num_scalar_prefetch=2,
        grid=(nb,),
        in_specs=[smem(lambda i, nu, be: (i,)),
                  smem(lambda i, nu, be: (jnp.minimum(i + 1, nb - 1),)),
                  pl.BlockSpec(memory_space=pl.ANY),
                  pl.BlockSpec((None, D, 2 * D_FF), lambda i, nu, be: (be[i], 0, 0)),
                  pl.BlockSpec((None, 1, 2 * D_FF), lambda i, nu, be: (be[i], 0, 0)),
                  pl.BlockSpec((None, D_FF, D), lambda i, nu, be: (be[i], 0, 0)),
                  pl.BlockSpec((None, 1, D), lambda i, nu, be: (be[i], 0, 0))],
        out_specs=pl.BlockSpec((tm, D), lambda i, nu, be: (i, 0)),
        scratch_shapes=[pltpu.VMEM((2, tm, D), F32), pltpu.SemaphoreType.DMA((2,))],
    )
    return pl.pallas_call(
        functools.partial(_moe_body, tm=tm),
        grid_spec=grid_spec,
        out_shape=jax.ShapeDtypeStruct((R, D), F32),
        compiler_params=_params(("arbitrary",)),
        name="moe_experts",
    )(n_used, blk_exp, row_tok, row_tok, h2, wu_bf, bu.reshape(N_EXPERTS, 1, 2 * D_FF),
      wd_bf, bd.reshape(N_EXPERTS, 1, D))


def _route(top_i, tm):
    T = top_i.shape[0]
    A = T * TOP_K
    nb = -(-(A + N_EXPERTS * (tm - 1)) // tm)
    member = (top_i[:, :, None] == jnp.arange(N_EXPERTS, dtype=jnp.int32)[None, None, :])
    member = jnp.sum(member.astype(jnp.int32), axis=1)
    counts = jnp.sum(member, axis=0)
    rank = jnp.cumsum(member, axis=0) - member
    padded = (counts + tm - 1) // tm * tm
    pad_end = jnp.cumsum(padded)
    pad_start = pad_end - padded
    dest = pad_start[top_i] + jnp.take_along_axis(rank, top_i, axis=1)
    tok = jnp.broadcast_to(jnp.arange(T, dtype=jnp.int32)[:, None], (T, TOP_K))
    row_tok = jnp.zeros((nb * tm,), jnp.int32).at[dest.reshape(-1)].set(tok.reshape(-1))
    blk_exp = jnp.minimum(jnp.searchsorted(pad_end, jnp.arange(nb, dtype=jnp.int32) * tm, side='right'),
                          N_EXPERTS - 1).astype(jnp.int32)
    n_used = (pad_end[-1:] // tm).astype(jnp.int32)
    return dest.astype(jnp.int32), row_tok, blk_exp, n_used


def _combine_body(dst_ref, dstn_ref, yb_hbm, gate_ref, x1_ref, g2_ref, shf_ref, scf_ref, gf_ref,
                  out_ref, buf, sem, *, tm):
    i = pl.program_id(0)
    nsteps = pl.num_programs(0)

    def gather(d_ref, slot):
        def issue(r, carry):
            for k in range(TOP_K):
                pltpu.make_async_copy(yb_hbm.at[pl.ds(d_ref[r * TOP_K + k], 1)],
                                      buf.at[slot, k, pl.ds(r, 1)], sem.at[slot]).start()
            return carry
        lax.fori_loop(0, tm, issue, 0, unroll=2)

    @pl.when(i == 0)
    def _():
        gather(dst_ref, 0)

    @pl.when(i + 1 < nsteps)
    def _():
        gather(dstn_ref, (i + 1) % 2)

    slot = i % 2
    for k in range(TOP_K):
        pltpu.make_async_copy(yb_hbm.at[pl.ds(0, tm)], buf.at[slot, k], sem.at[slot]).wait()
    gates = gate_ref[...]
    y = gates[:, 0:1] * buf[slot, 0]
    for k in range(1, TOP_K):
        y = y + gates[:, k:k + 1] * buf[slot, k]
    x2 = x1_ref[...] + g2_ref[...] * y
    ms = jnp.mean(x2 * x2, axis=-1, keepdims=True)
    h = x2 * lax.rsqrt(ms + EPS) * gf_ref[...]
    out_ref[...] = h * (1.0 + scf_ref[...]) + shf_ref[...]


def _combine(dest, yb, gates, x1, g2, shf, scf, gf, tm, tiles_per_group):
    T, D = x1.shape
    nsteps = T // tm
    dflat = dest.reshape(-1)
    smem = lambda f: pl.BlockSpec((tm * TOP_K,), f, memory_space=pltpu.SMEM)
    row = lambda w: pl.BlockSpec((tm, w), lambda i: (i, 0))
    ms = lambda m: _mod_spec(m, tm, tiles_per_group)
    return pl.pallas_call(
        functools.partial(_combine_body, tm=tm),
        grid=(nsteps,),
        in_specs=[smem(lambda i: (i,)), smem(lambda i: (jnp.minimum(i + 1, nsteps - 1),)),
                  pl.BlockSpec(memory_space=pl.ANY), row(TOP_K), row(D), ms(g2), ms(shf), ms(scf),
                  pl.BlockSpec((1, D), lambda i: (0, 0))],
        out_specs=row(D),
        out_shape=jax.ShapeDtypeStruct((T, D), F32),
        scratch_shapes=[pltpu.VMEM((2, TOP_K, tm, D), F32), pltpu.SemaphoreType.DMA((2,))],
        compiler_params=_params(("arbitrary",)),
        name="combine_final",
    )(dflat, dflat, yb, gates, x1, g2, shf, scf, gf.reshape(1, D))


def _group_mods(mod, n, rows_per_group, per_row):
    G = mod.shape[0]
    outs = []
    for j in range(n):
        m = mod[:, j * D_MODEL:(j + 1) * D_MODEL]
        if per_row:
            outs.append(jnp.repeat(m, rows_per_group, axis=0)[None])
        else:
            outs.append(m[:, None, :])
    return outs


def _trunk(x, mods, fmods, pos, weights, mixer, per_row):
    B, L, D = x.shape
    T = B * L
    (n1, w_in_bf, beta_ret, beta_att, w_out_bf, n2, rw, rb, wu_bf, bu, wd_bf, bd, gf) = weights
    tm = min(ROW_TILE, T)
    tiles_per_group = T // tm if per_row else L // tm
    sh1, sc1, g1, sh2, sc2, g2 = _group_mods(mods, N_MOD, L, per_row)
    shf, scf = _group_mods(fmods, 2, L, per_row)
    x2d = x.reshape(T, D)
    cos_t, sin_t = _rope_tables(pos)
    if per_row:
        cos_t, sin_t = jnp.tile(cos_t, (B, 1)), jnp.tile(sin_t, (B, 1))
    qr, kr, vr, gr, qa, ka, va = _inproj(x2d, sh1, sc1, n1, w_in_bf, cos_t, sin_t, tm, tiles_per_group)
    o_r, o_a, new_state = mixer(qr, kr, vr, qa, ka, va)
    x1, h2, top_i, gates = _merge(o_r, gr, o_a, x2d, g1, sh2, sc2, beta_ret, beta_att, w_out_bf,
                                  n2, rw, rb, tm, tiles_per_group)
    tme = min(MOE_TILE, T)
    dest, row_tok, blk_exp, n_used = _route(top_i, tme)
    yb = _moe_experts(h2, row_tok, blk_exp, n_used, wu_bf, bu, wd_bf, bd, tme)
    tmc = min(COMBINE_TILE, T)
    tiles_c = T // tmc if per_row else L // tmc
    if per_row and tmc != tm:
        raise ValueError("per-row modulation needs one tile")
    y = _combine(dest, yb, gates, x1, g2, shf, scf, gf, tmc, tiles_c)
    return y.reshape(B, L, D), new_state


def kernel(x_prompt, x_sample, state_ret, cache_win_k, cache_win_v, c_prompt, c_sample,
           ada_w, ada_b, norm1_g, w_in, beta_ret, beta_att, w_out, norm2_g,
           router_w, router_b, w_up, b_up, w_down, b_down,
           final_ada_w, final_ada_b, final_norm_g):
    B, S, D = x_prompt.shape
    DB, L = x_sample.shape[:2]
    assert ada_w.shape[0] == 1, "single-layer trunk"
    c_all = jnp.concatenate([c_prompt, c_sample], axis=0)
    pad = (-c_all.shape[0]) % 8
    c_all = jnp.pad(c_all, ((0, pad), (0, 0)))
    mods = _adaln(c_all, ada_w[0], ada_b[0])
    fmods = _adaln(c_all, final_ada_w, final_ada_b)
    weights = (norm1_g[0], w_in[0].astype(BF16), beta_ret[0], beta_att[0], w_out[0].astype(BF16),
               norm2_g[0], router_w[0], router_b[0], w_up[0].astype(BF16), b_up[0],
               w_down[0].astype(BF16), b_down[0], final_norm_g)

    def prompt_mixer(qr, kr, vr, qa, ka, va):
        s0 = jnp.zeros((B, H_RET, DK_RET, DV_RET), F32)
        o_r, s_fin = _retention(qr, kr, vr, s0, B, S, RET_CHUNK)
        k3, v3 = ka.reshape(B, S, ATT_W), va.reshape(B, S, ATT_W)
        o_a = _dilated_prompt(qa.reshape(B, S, ATT_W), k3, v3)
        keep = min(MAX_WINDOW, S)
        win = lambda t: t[:, S - keep:].reshape(B, keep, H_ATT, HD_ATT)
        return o_r, o_a.reshape(B * S, ATT_W), (s_fin, win(k3), win(v3))

    def sample_mixer(qr, kr, vr, qa, ka, va):
        o_r, s_new = _retention(qr, kr, vr, state_ret[0], DB, L, L)
        WB = cache_win_k.shape[2]
        o_a, nk, nv = _sample_attention(qa, ka, va, cache_win_k[0].reshape(DB, WB, ATT_W),
                                        cache_win_v[0].reshape(DB, WB, ATT_W), DB, L)
        shp = (DB, WB, H_ATT, HD_ATT)
        return o_r, o_a, (s_new, nk.reshape(shp), nv.reshape(shp))

    y_p, (sp, kp, vp) = _trunk(x_prompt, mods[:B], fmods[:B], jnp.arange(S), weights, prompt_mixer, False)
    y_s, (ss, ks_, vs_) = _trunk(x_sample, mods[B:B + DB], fmods[B:B + DB], PAST_LEN + jnp.arange(L),
                                 weights, sample_mixer, True)
    return (y_p, y_s, sp[None], ss[None], kp[None], vp[None], ks_[None], vs_[None])
```

```python
import functools
import math

import numpy as np
import jax
import jax.numpy as jnp
from jax import lax
from jax.experimental import pallas as pl
from jax.experimental.pallas import tpu as pltpu

F32 = jnp.float32
BF16 = jnp.bfloat16
HIGHEST = lax.Precision.HIGHEST

D_MODEL = 1024
PAST_LEN = 16384
H_RET, DK_RET, DV_RET = 4, 64, 128
RET_CHUNK = 128
H_ATT, HD_ATT = 8, 64
DILATED_PAIRS = ((128, 1), (512, 4), (2048, 16))
ATT_BLOCK = 128
MAX_WINDOW = 2048
ROPE_THETA = 10000.0
RET_QK = H_RET * DK_RET
RET_V = H_RET * DV_RET
ATT_W = H_ATT * HD_ATT
IN_WIDTH = 2 * RET_QK + 2 * RET_V + 3 * ATT_W
N_EXPERTS = 32
TOP_K = 4
D_FF = 1024
SWIGLU_LIMIT = 7.0
SWIGLU_ALPHA = 1.702
N_MOD = 6
EPS = 1e-6

LANES = 128
VMEM_LIMIT = 56 * 1024 * 1024
ROW_TILE = 512
MOE_TILE = 512
TOKEN_TILE = 256
N_SLABS = ATT_W // LANES
ATT_SPAN = 2048

RET_LOG_DECAY = tuple(math.log(1.0 - 2.0 ** (-5.0 - h)) for h in range(H_RET))


def _params(sem, vmem=VMEM_LIMIT):
    return pltpu.CompilerParams(dimension_semantics=sem, vmem_limit_bytes=vmem)


def _sigmoid(z):
    return 1.0 / (1.0 + jnp.exp(-z))


def _adaln_body(c_ref, w_ref, b_ref, o_ref):
    c = c_ref[...]
    s = c * _sigmoid(c)
    o_ref[...] = jnp.dot(s, w_ref[...], precision=HIGHEST, preferred_element_type=F32) + b_ref[...]


def _adaln(c, w, b):
    R, D = c.shape
    N = w.shape[1]
    tn = 1024
    return pl.pallas_call(
        _adaln_body,
        grid=(N // tn,),
        in_specs=[pl.BlockSpec((R, D), lambda j: (0, 0)),
                  pl.BlockSpec((D, tn), lambda j: (0, j)),
                  pl.BlockSpec((1, tn), lambda j: (0, j))],
        out_specs=pl.BlockSpec((R, tn), lambda j: (0, j)),
        out_shape=jax.ShapeDtypeStruct((R, N), F32),
        compiler_params=_params(("arbitrary",)),
        name="adaln",
    )(c, w, b.reshape(1, N))


def _rope_chunk(xc, cos, sin_signed, first_half):
    partner = jnp.where(first_half, pltpu.roll(xc, 96, 1), pltpu.roll(xc, 32, 1))
    return xc * cos + partner * sin_signed


def _inproj_body(x_ref, sh_ref, sc_ref, g_ref, w_ref, cos_ref, sin_ref, *refs, dilated):
    x = x_ref[...]
    tm = x.shape[0]
    ms = jnp.mean(x * x, axis=-1, keepdims=True)
    h = x * lax.rsqrt(ms + EPS) * g_ref[...]
    h = h * (1.0 + sc_ref[...]) + sh_ref[...]
    p = jnp.dot(h.astype(BF16), w_ref[...], preferred_element_type=F32)
    cos = cos_ref[...]
    sin = sin_ref[...]
    first_half = (lax.broadcasted_iota(jnp.int32, (1, LANES), 1) % HD_ATT) < (HD_ATT // 2)

    def chunk(off, c, rope):
        xc = p[:, off + c * LANES: off + (c + 1) * LANES]
        return _rope_chunk(xc, cos, sin, first_half) if rope else xc

    qr_ref, kr_ref, vr_ref, gr_ref = refs[:4]
    o = 0
    for c in range(RET_QK // LANES):
        qr_ref[:, c * LANES:(c + 1) * LANES] = chunk(o, c, True)
    o += RET_QK
    for c in range(RET_QK // LANES):
        kr_ref[:, c * LANES:(c + 1) * LANES] = chunk(o, c, True) * (DK_RET ** -0.5)
    o += RET_QK
    vr_ref[...] = p[:, o:o + RET_V]; o += RET_V
    gr_ref[...] = p[:, o:o + RET_V]; o += RET_V
    if not dilated:
        qa_ref, ka_ref, va_ref = refs[4:7]
        for ref, rope in ((qa_ref, True), (ka_ref, True), (va_ref, False)):
            for c in range(N_SLABS):
                ref[:, c * LANES:(c + 1) * LANES] = chunk(o, c, rope)
            o += ATT_W
        return
    ka_ref, va_ref = refs[4:6]
    slab_refs = refs[6:15]
    scr, scr4 = refs[15:17]
    n4 = tm // 4
    n16 = tm // 16
    for t, (nat_ref, rope, scale) in enumerate(((None, True, HD_ATT ** -0.5), (ka_ref, True, 1.0),
                                                (va_ref, False, 1.0))):
        s1_ref, s4_ref, s16_ref = slab_refs[t], slab_refs[3 + t], slab_refs[6 + t]
        for c in range(N_SLABS):
            xc = chunk(o, c, rope)
            if nat_ref is not None:
                nat_ref[:, c * LANES:(c + 1) * LANES] = xc
            if scale != 1.0:
                xc = xc * scale
            scr[...] = xc
            s1_ref[c] = xc.astype(BF16)
            for r in range(4):
                part = scr[pl.ds(r, n4, stride=4), :]
                s4_ref[c, r] = part.astype(BF16)
                scr4[r] = part
            for r in range(4):
                for r2 in range(4):
                    s16_ref[c, 4 * r2 + r] = scr4[r, pl.ds(r2, n16, stride=4), :].astype(BF16)
        o += ATT_W


def _mod_spec(mod, tm, tiles_per_group):
    R = mod.shape[1]
    return pl.BlockSpec((None, R, mod.shape[2]), lambda i: (i // tiles_per_group, 0, 0))


def _inproj(x2d, sh, sc, g, w_bf, cos_t, sin_t, tm, tiles_per_group, dilated, B=1):
    T, D = x2d.shape
    pos_tiles = cos_t.shape[0] // tm
    row = lambda w: pl.BlockSpec((tm, w), lambda i: (i, 0))
    tab = pl.BlockSpec((tm, LANES), lambda i: (i % pos_tiles, 0))
    f32 = lambda w: jax.ShapeDtypeStruct((T, w), F32)
    if dilated:
        S = T // B
        tpb = S // tm
        widths = (RET_QK, RET_QK, RET_V, RET_V, ATT_W, ATT_W)
        out_specs = [row(w) for w in widths]
        out_shape = [f32(w) for w in widths]
        for dil in (1, 4, 16):
            for _ in range(3):
                if dil == 1:
                    out_specs.append(pl.BlockSpec((None, N_SLABS, tm, LANES),
                                                  lambda i: (i // tpb, 0, i % tpb, 0)))
                    out_shape.append(jax.ShapeDtypeStruct((B, N_SLABS, S, LANES), BF16))
                else:
                    out_specs.append(pl.BlockSpec((None, N_SLABS, dil, tm // dil, LANES),
                                                  lambda i: (i // tpb, 0, 0, i % tpb, 0)))
                    out_shape.append(jax.ShapeDtypeStruct((B, N_SLABS, dil, S // dil, LANES), BF16))
        scratch = [pltpu.VMEM((tm, LANES), F32), pltpu.VMEM((4, tm // 4, LANES), F32)]
    else:
        widths = (RET_QK, RET_QK, RET_V, RET_V, ATT_W, ATT_W, ATT_W)
        out_specs = [row(w) for w in widths]
        out_shape = [f32(w) for w in widths]
        scratch = []
    return pl.pallas_call(
        functools.partial(_inproj_body, dilated=dilated),
        grid=(T // tm,),
        in_specs=[row(D), _mod_spec(sh, tm, tiles_per_group), _mod_spec(sc, tm, tiles_per_group),
                  pl.BlockSpec((1, D), lambda i: (0, 0)),
                  pl.BlockSpec((D, IN_WIDTH), lambda i: (0, 0)), tab, tab],
        out_specs=out_specs, out_shape=out_shape, scratch_shapes=scratch,
        compiler_params=_params(("arbitrary",)),
        name="inproj",
    )(x2d, sh, sc, g.reshape(1, D), w_bf, cos_t, sin_t)


def _rope_tables(pos):
    half = HD_ATT // 2
    inv = ROPE_THETA ** (-jnp.arange(half, dtype=F32) / half)
    ang = pos.astype(F32)[:, None] * inv[None, :]
    cos = jnp.cos(ang)
    sin = jnp.sin(ang)
    cos_t = jnp.tile(cos, (1, LANES // half))
    sin_t = jnp.tile(jnp.concatenate([-sin, sin], axis=1), (1, LANES // HD_ATT))
    return cos_t, sin_t


def _retention_body(q_ref, k_ref, v_ref, s0_ref, o_ref, sfin_ref, s_scr, *, L):
    c = pl.program_id(1)

    @pl.when(c == 0)
    def _():
        s_scr[...] = s0_ref[...]

    ii = lax.broadcasted_iota(jnp.int32, (L, L), 0)
    jj = lax.broadcasted_iota(jnp.int32, (L, L), 1)
    diff = (ii - jj).astype(F32)
    causal = diff >= 0
    row = lax.broadcasted_iota(jnp.int32, (L, 1), 0).astype(F32)
    for h in range(H_RET):
        lg = RET_LOG_DECAY[h]
        decay = jnp.where(causal, jnp.exp(jnp.where(causal, diff, 0.0) * lg), 0.0)
        q = q_ref[:, h * DK_RET:(h + 1) * DK_RET]
        k = k_ref[:, h * DK_RET:(h + 1) * DK_RET]
        v = v_ref[:, h * DV_RET:(h + 1) * DV_RET]
        qb, vb = q.astype(BF16), v.astype(BF16)
        scores = lax.dot_general(qb, k.astype(BF16), (((1,), (1,)), ((), ())),
                                 preferred_element_type=F32) * decay
        S = s_scr[h]
        o = jnp.dot(scores.astype(BF16), vb, preferred_element_type=F32)
        o = o + jnp.dot(qb, S.astype(BF16), preferred_element_type=F32) * jnp.exp((row + 1.0) * lg)
        k_dec = k * jnp.exp((L - 1.0 - row) * lg)
        s_scr[h] = math.exp(L * lg) * S + lax.dot_general(
            k_dec.astype(BF16), vb, (((0,), (0,)), ((), ())), preferred_element_type=F32)
        o_ref[:, h * DV_RET:(h + 1) * DV_RET] = o

    @pl.when(c == pl.num_programs(1) - 1)
    def _():
        sfin_ref[...] = s_scr[...]


def _retention(q2d, k2d, v2d, s0, B, L_seq, L):
    nc = L_seq // L
    row = lambda w: pl.BlockSpec((L, w), lambda b, c: (b * nc + c, 0))
    st = pl.BlockSpec((None, H_RET, DK_RET, DV_RET), lambda b, c: (b, 0, 0, 0))
    return pl.pallas_call(
        functools.partial(_retention_body, L=L),
        grid=(B, nc),
        in_specs=[row(RET_QK), row(RET_QK), row(RET_V), st],
        out_specs=[row(RET_V), st],
        out_shape=[jax.ShapeDtypeStruct((B * L_seq, RET_V), F32),
                   jax.ShapeDtypeStruct((B, H_RET, DK_RET, DV_RET), F32)],
        scratch_shapes=[pltpu.VMEM((H_RET, DK_RET, DV_RET), F32)],
        compiler_params=_params(("arbitrary", "arbitrary")),
        name="retention",
    )(q2d, k2d, v2d, s0)


def _band_attention(q, kcat, vcat, valid, low):
    nt = (((1,), (1,)), ((), ()))
    zero = jnp.zeros_like(q)
    outs = []
    for hh in range(2):
        qh = jnp.where(low if hh == 0 else ~low, q, zero)
        s = lax.dot_general(qh, kcat, nt, preferred_element_type=F32)
        s = jnp.where(valid, s, -jnp.inf)
        m = jnp.max(s, axis=-1, keepdims=True)
        p = jnp.exp(s - m)
        l = jnp.sum(p, axis=-1, keepdims=True)
        outs.append((jnp.dot(p.astype(BF16), vcat, preferred_element_type=F32), m, l))
    (a0, m0, l0), (a1, m1, l1) = outs
    shape = a0.shape
    return (jnp.where(low, a0, a1), jnp.where(low, jnp.broadcast_to(m0, shape), jnp.broadcast_to(m1, shape)),
            jnp.where(low, jnp.broadcast_to(l0, shape), jnp.broadcast_to(l1, shape)))


def _dilated_body(q1, k1, k1p, v1, v1p, q4, k4, k4p, v4, v4p, q16, k16, k16p, v16, v16p,
                  o_ref, kf1, vf1, kf4, vf4, acc_s, m_s, l_s):
    j = pl.program_id(2)
    nq = ATT_BLOCK
    n1 = ATT_SPAN // nq
    GROUP = 4
    kf1[0:nq] = k1p[...]
    kf1[nq:] = k1[...]
    vf1[0:nq] = v1p[...]
    vf1[nq:] = v1[...]
    for r in range(4):
        kf4[r, 0:nq] = k4p[r]
        kf4[r, nq:] = k4[r]
        vf4[r, 0:nq] = v4p[r]
        vf4[r, nq:] = v4[r]
    qi = lax.broadcasted_iota(jnp.int32, (nq, 2 * nq), 0) + nq
    ki = lax.broadcasted_iota(jnp.int32, (nq, 2 * nq), 1)
    dist = qi - ki
    band = (dist >= 0) & (dist <= nq)
    in_cur = ki >= nq
    low = lax.broadcasted_iota(jnp.int32, (1, LANES), 1) < HD_ATT

    def mask(has_prev):
        return band & (has_prev | in_cur)

    def merge(rows, a, m, l):
        m_old, l_old = m_s[rows, :], l_s[rows, :]
        m_new = jnp.maximum(m_old, m)
        w_old, w_new = jnp.exp(m_old - m_new), jnp.exp(m - m_new)
        return acc_s[rows, :] * w_old + a * w_new, m_new, l_old * w_old + l * w_new

    def dil1(g, carry):
        res = []
        for u in range(GROUP):
            qb = g * GROUP + u
            r0 = pl.multiple_of(qb * nq, nq)
            res.append((pl.ds(r0, nq), _band_attention(
                q1[pl.ds(r0, nq), :], kf1[pl.ds(r0, 2 * nq), :], vf1[pl.ds(r0, 2 * nq), :],
                mask((j > 0) | (qb > 0)), low)))
        for rows, (a, m, l) in res:
            acc_s[rows, :] = a
            m_s[rows, :] = m
            l_s[rows, :] = l
        return carry

    lax.fori_loop(0, n1 // GROUP, dil1, 0)

    def dil4(qb, carry):
        r0 = pl.multiple_of(qb * nq, nq)
        res = []
        for r in range(4):
            res.append((pl.ds(qb * (4 * nq) + r, nq, stride=4), _band_attention(
                q4[r, pl.ds(r0, nq), :], kf4[r, pl.ds(r0, 2 * nq), :], vf4[r, pl.ds(r0, 2 * nq), :],
                mask((j > 0) | (qb > 0)), low)))
        for rows, (a, m, l) in res:
            a, m, l = merge(rows, a, m, l)
            acc_s[rows, :] = a
            m_s[rows, :] = m
            l_s[rows, :] = l
        return carry

    lax.fori_loop(0, ATT_SPAN // 4 // nq, dil4, 0)

    def dil16(g, carry):
        res = []
        for u in range(GROUP):
            r = g * GROUP + u
            kcat = jnp.concatenate([k16p[r], k16[r]], axis=0)
            vcat = jnp.concatenate([v16p[r], v16[r]], axis=0)
            res.append((pl.ds(r, nq, stride=16), _band_attention(q16[r], kcat, vcat, mask(j > 0), low)))
        for rows, (a, m, l) in res:
            a, m, l = merge(rows, a, m, l)
            o_ref[rows, :] = a / l
        return carry

    lax.fori_loop(0, 16 // GROUP, dil16, 0)


def _dilated_prompt(slabs, B, S):
    q1, k1, v1, q4, k4, v4, q16, k16, v16 = slabs
    assert S % ATT_SPAN == 0 and DILATED_PAIRS == ((128, 1), (512, 4), (2048, 16))
    nq = ATT_BLOCK
    prev = lambda j, per: jnp.maximum(j * per - 1, 0)
    s1 = pl.BlockSpec((None, None, ATT_SPAN, LANES), lambda b, c, j: (b, c, j, 0))
    s1p = pl.BlockSpec((None, None, nq, LANES), lambda b, c, j: (b, c, prev(j, ATT_SPAN // nq), 0))
    s4 = pl.BlockSpec((None, None, 4, ATT_SPAN // 4, LANES), lambda b, c, j: (b, c, 0, j, 0))
    s4p = pl.BlockSpec((None, None, 4, nq, LANES), lambda b, c, j: (b, c, 0, prev(j, ATT_SPAN // 4 // nq), 0))
    s16 = pl.BlockSpec((None, None, 16, nq, LANES), lambda b, c, j: (b, c, 0, j, 0))
    s16p = pl.BlockSpec((None, None, 16, nq, LANES), lambda b, c, j: (b, c, 0, prev(j, 1), 0))
    return pl.pallas_call(
        _dilated_body,
        grid=(B, N_SLABS, S // ATT_SPAN),
        in_specs=[s1, s1, s1p, s1, s1p, s4, s4, s4p, s4, s4p, s16, s16, s16p, s16, s16p],
        out_specs=pl.BlockSpec((None, None, ATT_SPAN, LANES), lambda b, c, j: (b, c, j, 0)),
        out_shape=jax.ShapeDtypeStruct((B, N_SLABS, S, LANES), F32),
        scratch_shapes=[pltpu.VMEM((ATT_SPAN + nq, LANES), BF16), pltpu.VMEM((ATT_SPAN + nq, LANES), BF16),
                        pltpu.VMEM((4, ATT_SPAN // 4 + nq, LANES), BF16),
                        pltpu.VMEM((4, ATT_SPAN // 4 + nq, LANES), BF16),
                        pltpu.VMEM((ATT_SPAN, LANES), F32), pltpu.VMEM((ATT_SPAN, LANES), F32),
                        pltpu.VMEM((ATT_SPAN, LANES), F32)],
        compiler_params=_params(("arbitrary", "arbitrary", "arbitrary")),
        name="dilated_attention",
    )(q1, k1, k1, v1, v1, q4, k4, k4, v4, v4, q16, k16, k16, v16, v16)


def _sample_multiplicity(L, WB):
    pad = LANES
    cnt = np.zeros((L, WB + pad), np.float32)
    for l in range(L):
        for window, dil in DILATED_PAIRS:
            for j in range(window // dil + 1):
                idx = WB + l - dil * j
                if idx >= 0:
                    cnt[l, idx] += 1.0
    cnt = np.tile(cnt, (H_ATT, 1))
    return cnt[:, :WB], cnt[:, WB:]


def _sample_att_body(q_ref, kn_ref, vn_ref, bk_ref, bv_ref, cb_ref, cn_ref,
                     o_ref, nk_ref, nv_ref, *, L, WB):
    q = q_ref[...]
    lane_head = lax.broadcasted_iota(jnp.int32, (1, ATT_W), 1) // HD_ATT
    qexp = jnp.concatenate([jnp.where(lane_head == h, q, 0.0) for h in range(H_ATT)], axis=0).astype(BF16)
    zpad = jnp.zeros((LANES - L, ATT_W), F32)
    kn = jnp.concatenate([kn_ref[...], zpad], axis=0).astype(BF16)
    vn = jnp.concatenate([vn_ref[...], zpad], axis=0).astype(BF16)
    bk = bk_ref[...]
    bv = bv_ref[...]
    nt = (((1,), (1,)), ((), ()))
    scale = HD_ATT ** -0.5
    sb = lax.dot_general(qexp, bk.astype(BF16), nt, preferred_element_type=F32) * scale
    sn = lax.dot_general(qexp, kn, nt, preferred_element_type=F32) * scale
    cb = cb_ref[...]
    cn = cn_ref[...]
    sb = jnp.where(cb > 0, sb, -jnp.inf)
    sn = jnp.where(cn > 0, sn, -jnp.inf)
    m = jnp.maximum(jnp.max(sb, axis=-1, keepdims=True), jnp.max(sn, axis=-1, keepdims=True))
    pb = cb * jnp.exp(sb - m)
    pn = cn * jnp.exp(sn - m)
    l = jnp.sum(pb, axis=-1, keepdims=True) + jnp.sum(pn, axis=-1, keepdims=True)
    o_all = (jnp.dot(pb.astype(BF16), bv.astype(BF16), preferred_element_type=F32)
             + jnp.dot(pn.astype(BF16), vn, preferred_element_type=F32)) / l
    o = jnp.zeros((L, ATT_W), F32)
    for h in range(H_ATT):
        o = o + jnp.where(lane_head == h, o_all[h * L:(h + 1) * L], 0.0)
    for c in range(N_SLABS):
        o_ref[c] = o[:, c * LANES:(c + 1) * LANES]
    nk_ref[0:WB - L, :] = bk[L:WB]
    nk_ref[WB - L:WB, :] = kn_ref[...]
    nv_ref[0:WB - L, :] = bv[L:WB]
    nv_ref[WB - L:WB, :] = vn_ref[...]


def _sample_attention(q2d, k2d, v2d, buf_k, buf_v, DB, L):
    WB = buf_k.shape[1]
    cb, cn = _sample_multiplicity(L, WB)
    row = pl.BlockSpec((L, ATT_W), lambda b: (b, 0))
    buf = pl.BlockSpec((None, WB, ATT_W), lambda b: (b, 0, 0))
    const = lambda a: pl.BlockSpec(a.shape, lambda b: (0, 0))
    return pl.pallas_call(
        functools.partial(_sample_att_body, L=L, WB=WB),
        grid=(DB,),
        in_specs=[row, row, row, buf, buf, const(cb), const(cn)],
        out_specs=[pl.BlockSpec((N_SLABS, L, LANES), lambda b: (0, b, 0)), buf, buf],
        out_shape=[jax.ShapeDtypeStruct((N_SLABS, DB * L, LANES), F32),
                   jax.ShapeDtypeStruct((DB, WB, ATT_W), F32),
                   jax.ShapeDtypeStruct((DB, WB, ATT_W), F32)],
        compiler_params=_params(("arbitrary",)),
        name="sample_attention",
    )(q2d, k2d, v2d, buf_k, buf_v, jnp.asarray(cb), jnp.asarray(cn))


def _group_mean(sq, width):
    if width == LANES:
        return jnp.mean(sq, axis=-1, keepdims=True)
    lane = lax.broadcasted_iota(jnp.int32, (1, LANES), 1)
    low = lane < width
    s_lo = jnp.sum(jnp.where(low, sq, 0.0), axis=-1, keepdims=True)
    s_hi = jnp.sum(jnp.where(low, 0.0, sq), axis=-1, keepdims=True)
    return jnp.where(low, s_lo, s_hi) * (1.0 / width)


def _merge_body(or_ref, gr_ref, oa0, oa1, oa2, oa3, x_ref, g1_ref, sh_ref, sc_ref, br_ref, ba_ref, wo_ref,
                n2_ref, rw_ref, rb_ref, cin_ref,
                x1_ref, h2_ref, ti_ref, gt_ref, rk_ref, cnt_ref, cnt_scr):
    i = pl.program_id(0)

    @pl.when(i == 0)
    def _():
        cnt_scr[...] = cin_ref[...]

    parts = []
    for c in range(RET_V // LANES):
        cs = slice(c * LANES, (c + 1) * LANES)
        o = or_ref[:, cs]
        o = o * lax.rsqrt(_group_mean(o * o, DV_RET) + EPS) * br_ref[:, cs]
        g = gr_ref[:, cs]
        parts.append((o * (g * _sigmoid(g))).astype(BF16))
    for c, oa in enumerate((oa0, oa1, oa2, oa3)):
        cs = slice(c * LANES, (c + 1) * LANES)
        o = oa[...]
        o = o * lax.rsqrt(_group_mean(o * o, HD_ATT) + EPS) * ba_ref[:, cs]
        parts.append(o.astype(BF16))
    mix = jnp.concatenate(parts, axis=1)
    y = jnp.dot(mix, wo_ref[...], preferred_element_type=F32)
    x1 = x_ref[...] + g1_ref[...] * y
    x1_ref[...] = x1
    ms = jnp.mean(x1 * x1, axis=-1, keepdims=True)
    h2 = x1 * lax.rsqrt(ms + EPS) * n2_ref[...]
    h2 = h2 * (1.0 + sc_ref[...]) + sh_ref[...]
    h2_ref[...] = h2
    logits = jnp.dot(h2, rw_ref[...], precision=HIGHEST, preferred_element_type=F32) + rb_ref[...]
    tm = logits.shape[0]
    lane = lax.broadcasted_iota(jnp.int32, logits.shape, 1).astype(F32)
    vals, idxs = [], []
    for _ in range(TOP_K):
        m = jnp.max(logits, axis=-1, keepdims=True)
        idx = jnp.min(jnp.where(logits == m, lane, float(N_EXPERTS)), axis=-1, keepdims=True)
        vals.append(m)
        idxs.append(idx)
        logits = jnp.where(lane == idx, -jnp.inf, logits)
    e = [jnp.exp(v - vals[0]) for v in vals]
    tot = e[0] + e[1] + e[2] + e[3]
    member = jnp.zeros(logits.shape, F32)
    for k in range(TOP_K):
        member = member + jnp.where(lane == idxs[k], 1.0, 0.0)
    before = (lax.broadcasted_iota(jnp.int32, (tm, tm), 0) > lax.broadcasted_iota(jnp.int32, (tm, tm), 1))
    rank = jnp.dot(jnp.where(before, 1.0, 0.0).astype(BF16), member.astype(BF16),
                   preferred_element_type=F32) + cnt_scr[...]
    for k in range(TOP_K):
        ti_ref[:, k:k + 1] = idxs[k].astype(jnp.int32)
        gt_ref[:, k:k + 1] = e[k] / tot
        rk_ref[:, k:k + 1] = jnp.sum(jnp.where(lane == idxs[k], rank, 0.0), axis=-1,
                                     keepdims=True).astype(jnp.int32)
    cnt = cnt_scr[...] + jnp.sum(member, axis=0, keepdims=True)
    cnt_scr[...] = cnt
    cnt_ref[...] = cnt


def _merge(o_r, g_r, o_a, x2d, g1, sh2, sc2, beta_ret, beta_att, wo_bf, n2, rw, rb, cnt_in, tm,
           tiles_per_group):
    T, D = x2d.shape
    tpg = o_a.shape[2] // tm
    row = lambda w: pl.BlockSpec((tm, w), lambda i: (i, 0))
    const = lambda r, c: pl.BlockSpec((r, c), lambda i: (0, 0))
    ms = lambda m: _mod_spec(m, tm, tiles_per_group)
    slab = lambda c: pl.BlockSpec((None, None, tm, LANES), lambda i: (i // tpg, c, i % tpg, 0))
    ins = [o_r, g_r, o_a, o_a, o_a, o_a, x2d, g1, sh2, sc2, beta_ret.reshape(1, RET_V),
           beta_att.reshape(1, ATT_W), wo_bf, n2.reshape(1, D), rw, rb.reshape(1, N_EXPERTS), cnt_in]
    in_specs = [row(RET_V), row(RET_V), slab(0), slab(1), slab(2), slab(3), row(D), ms(g1), ms(sh2), ms(sc2),
                const(1, RET_V), const(1, ATT_W), const(D, D), const(1, D),
                const(D, N_EXPERTS), const(1, N_EXPERTS), const(1, N_EXPERTS)]
    return pl.pallas_call(
        _merge_body,
        grid=(T // tm,),
        in_specs=in_specs,
        out_specs=[row(D), row(D), row(TOP_K), row(TOP_K), row(TOP_K), const(1, N_EXPERTS)],
        out_shape=[jax.ShapeDtypeStruct((T, D), F32), jax.ShapeDtypeStruct((T, D), F32),
                   jax.ShapeDtypeStruct((T, TOP_K), jnp.int32), jax.ShapeDtypeStruct((T, TOP_K), F32),
                   jax.ShapeDtypeStruct((T, TOP_K), jnp.int32), jax.ShapeDtypeStruct((1, N_EXPERTS), F32)],
        scratch_shapes=[pltpu.VMEM((1, N_EXPERTS), F32)],
        compiler_params=_params(("arbitrary",)),
        name="merge_router",
    )(*ins)


def _route(top_i, rank, counts, tm):
    T = top_i.shape[0]
    A = T * TOP_K
    nb = -(-(A + N_EXPERTS * (tm - 1)) // tm)
    counts = counts.reshape(N_EXPERTS).astype(jnp.int32)
    padded = (counts + tm - 1) // tm * tm
    pad_end = jnp.cumsum(padded)
    pad_start = pad_end - padded
    onehot = top_i[:, :, None] == jnp.arange(N_EXPERTS, dtype=jnp.int32)[None, None, :]
    dest = rank + jnp.sum(jnp.where(onehot, pad_start[None, None, :], 0), axis=-1)
    blk0 = jnp.arange(nb, dtype=jnp.int32) * tm
    blk_exp = jnp.minimum(jnp.sum((pad_end[None, :] <= blk0[:, None]).astype(jnp.int32), axis=1),
                          N_EXPERTS - 1).astype(jnp.int32)
    n_used = (pad_end[-1:] // tm).astype(jnp.int32)
    blk_new = jnp.concatenate([jnp.ones((1,), jnp.int32),
                               (blk_exp[1:] != blk_exp[:-1]).astype(jnp.int32)])
    fill_start = jnp.concatenate([pad_start + counts, pad_end[-1:]]).astype(jnp.int32)
    fill_n = jnp.concatenate([padded - counts, nb * tm - pad_end[-1:]]).astype(jnp.int32)
    return dest.astype(jnp.int32), blk_exp, blk_new, n_used, fill_start, fill_n, nb


def _dispatch_body(fs_ref, fn_ref, dst_ref, hp_ref, hs_ref, xs_hbm, zbuf, sem, zsem, *, tm, fill_max,
                   prompt_tiles):
    i = pl.program_id(0)

    def scatter(h_ref):
        def issue(r, carry):
            for k in range(TOP_K):
                pltpu.make_async_copy(h_ref.at[pl.ds(r, 1)], xs_hbm.at[pl.ds(dst_ref[r * TOP_K + k], 1)],
                                      sem).start()
            return carry
        lax.fori_loop(0, tm, issue, 0, unroll=2)

    @pl.when(i < prompt_tiles)
    def _():
        scatter(hp_ref)

    @pl.when(i >= prompt_tiles)
    def _():
        scatter(hs_ref)

    @pl.when(i == pl.num_programs(0) - 1)
    def _():
        zbuf[...] = jnp.zeros(zbuf.shape, F32)

        def fill(e, wait):
            start = fs_ref[e]
            n = fn_ref[e]
            head = jnp.minimum((-start) & 7, n)

            def one(r, carry):
                cp = pltpu.make_async_copy(zbuf.at[pl.ds(0, 1)], xs_hbm.at[pl.ds(start + r, 1)], zsem)
                cp.wait() if wait else cp.start()
                return carry

            lax.fori_loop(0, head, one, 0)
            start = start + head
            n = n - head
            sz = fill_max
            while sz >= 8:
                @pl.when((n & sz) != 0)
                def _(start=start, sz=sz):
                    cp = pltpu.make_async_copy(zbuf.at[pl.ds(0, sz)],
                                               xs_hbm.at[pl.ds(pl.multiple_of(start, 8), sz)], zsem)
                    cp.wait() if wait else cp.start()
                start = start + (n & sz)
                sz //= 2

        def fill_all(wait):
            def body(e, carry):
                fill(e, wait)
                return carry
            lax.fori_loop(0, N_EXPERTS, body, 0)

            def tail(c, carry):
                row = pl.multiple_of(fs_ref[N_EXPERTS] + c * fill_max, fill_max)
                cp = pltpu.make_async_copy(zbuf, xs_hbm.at[pl.ds(row, fill_max)], zsem)
                cp.wait() if wait else cp.start()
                return carry
            lax.fori_loop(0, fn_ref[N_EXPERTS] // fill_max, tail, 0)

        fill_all(False)
        fill_all(True)

    for k in range(TOP_K):
        pltpu.make_async_copy(hp_ref, xs_hbm.at[pl.ds(0, tm)], sem).wait()


def _dispatch(h2_p, h2_s, dest, fill_start, fill_n, R, tm, fill_max):
    Tp, D = h2_p.shape
    Ts = h2_s.shape[0]
    np_, ns_ = Tp // tm, Ts // tm
    grid_spec = pltpu.PrefetchScalarGridSpec(
        num_scalar_prefetch=2,
        grid=(np_ + ns_,),
        in_specs=[pl.BlockSpec((tm * TOP_K,), lambda i, fs, fn: (i,), memory_space=pltpu.SMEM),
                  pl.BlockSpec((tm, D), lambda i, fs, fn: (jnp.minimum(i, np_ - 1), 0)),
                  pl.BlockSpec((tm, D), lambda i, fs, fn: (jnp.maximum(i - np_, 0), 0))],
        out_specs=pl.BlockSpec(memory_space=pl.ANY),
        scratch_shapes=[pltpu.VMEM((fill_max, D), F32), pltpu.SemaphoreType.DMA(()),
                        pltpu.SemaphoreType.DMA(())],
    )
    return pl.pallas_call(
        functools.partial(_dispatch_body, tm=tm, fill_max=fill_max, prompt_tiles=np_),
        grid_spec=grid_spec,
        out_shape=jax.ShapeDtypeStruct((R, D), F32),
        compiler_params=_params(("arbitrary",)),
        name="moe_dispatch",
    )(fill_start, fill_n, dest.reshape(-1), h2_p, h2_s)


def _moe_body(nused_ref, bexp_ref, new_ref, x_ref, wu_ref, bu_ref, wd_ref, bd_ref, y_ref, wu_bf, wd_bf):
    i = pl.program_id(0)

    @pl.when(new_ref[i] != 0)
    def _():
        wu_bf[...] = wu_ref[...].astype(BF16)
        wd_bf[...] = wd_ref[...].astype(BF16)

    @pl.when(i < nused_ref[0])
    def _():
        x = x_ref[...].astype(BF16)
        u = jnp.dot(x, wu_bf[...], preferred_element_type=F32) + bu_ref[...]
        x_glu = jnp.minimum(u[:, :D_FF], SWIGLU_LIMIT)
        x_lin = jnp.clip(u[:, D_FF:], -SWIGLU_LIMIT, SWIGLU_LIMIT)
        act = x_glu * _sigmoid(SWIGLU_ALPHA * x_glu) * (x_lin + 1.0)
        y_ref[...] = jnp.dot(act.astype(BF16), wd_bf[...], preferred_element_type=F32) + bd_ref[...]

    @pl.when(i >= nused_ref[0])
    def _():
        y_ref[...] = jnp.zeros(y_ref.shape, F32)


def _moe_experts(xs, blk_exp, blk_new, n_used, w_up, bu, w_down, bd, tm):
    R, D = xs.shape
    blk = lambda i, nu, be, nw: (jnp.minimum(i, nu[0] - 1), 0)
    wspec = lambda r, c: pl.BlockSpec((None, r, c), lambda i, nu, be, nw: (be[i], 0, 0))
    grid_spec = pltpu.PrefetchScalarGridSpec(
        num_scalar_prefetch=3,
        grid=(R // tm,),
        in_specs=[pl.BlockSpec((tm, D), blk), wspec(D, 2 * D_FF), wspec(1, 2 * D_FF),
                  wspec(D_FF, D), wspec(1, D)],
        out_specs=pl.BlockSpec((tm, D), lambda i, nu, be, nw: (i, 0)),
        scratch_shapes=[pltpu.VMEM((D, 2 * D_FF), BF16), pltpu.VMEM((D_FF, D), BF16)],
    )
    return pl.pallas_call(
        _moe_body,
        grid_spec=grid_spec,
        out_shape=jax.ShapeDtypeStruct((R, D), F32),
        compiler_params=_params(("arbitrary",)),
        name="moe_experts",
    )(n_used, blk_exp, blk_new, xs, w_up, bu.reshape(N_EXPERTS, 1, 2 * D_FF), w_down,
      bd.reshape(N_EXPERTS, 1, D))


def _combine_body(dst_ref, dstn_ref, yb_hbm, gate_ref, x1_ref, g2_ref, shf_ref, scf_ref, gf_ref,
                  out_ref, buf, sem, *, tm):
    i = pl.program_id(0)
    nsteps = pl.num_programs(0)

    def gather(d_ref, slot):
        def issue(r, carry):
            for k in range(TOP_K):
                pltpu.make_async_copy(yb_hbm.at[pl.ds(d_ref[r * TOP_K + k], 1)],
                                      buf.at[slot, k, pl.ds(r, 1)], sem.at[slot]).start()
            return carry
        lax.fori_loop(0, tm, issue, 0, unroll=2)

    @pl.when(i == 0)
    def _():
        gather(dst_ref, 0)

    @pl.when(i + 1 < nsteps)
    def _():
        gather(dstn_ref, (i + 1) % 2)

    slot = i % 2
    for k in range(TOP_K):
        pltpu.make_async_copy(yb_hbm.at[pl.ds(0, tm)], buf.at[slot, k], sem.at[slot]).wait()
    gates = gate_ref[...]
    y = gates[:, 0:1] * buf[slot, 0]
    for k in range(1, TOP_K):
        y = y + gates[:, k:k + 1] * buf[slot, k]
    x2 = x1_ref[...] + g2_ref[...] * y
    ms = jnp.mean(x2 * x2, axis=-1, keepdims=True)
    h = x2 * lax.rsqrt(ms + EPS) * gf_ref[...]
    out_ref[...] = h * (1.0 + scf_ref[...]) + shf_ref[...]


def _combine(dest, yb, gates, x1, g2, shf, scf, gf, tm, tiles_per_group):
    T, D = x1.shape
    nsteps = T // tm
    dflat = dest.reshape(-1)
    smem = lambda f: pl.BlockSpec((tm * TOP_K,), f, memory_space=pltpu.SMEM)
    row = lambda w: pl.BlockSpec((tm, w), lambda i: (i, 0))
    ms = lambda m: _mod_spec(m, tm, tiles_per_group)
    return pl.pallas_call(
        functools.partial(_combine_body, tm=tm),
        grid=(nsteps,),
        in_specs=[smem(lambda i: (i,)), smem(lambda i: (jnp.minimum(i + 1, nsteps - 1),)),
                  pl.BlockSpec(memory_space=pl.ANY), row(TOP_K), row(D), ms(g2), ms(shf), ms(scf),
                  pl.BlockSpec((1, D), lambda i: (0, 0))],
        out_specs=row(D),
        out_shape=jax.ShapeDtypeStruct((T, D), F32),
        scratch_shapes=[pltpu.VMEM((2, TOP_K, tm, D), F32), pltpu.SemaphoreType.DMA((2,))],
        compiler_params=_params(("arbitrary",)),
        name="combine_final",
    )(dflat, dflat, yb, gates, x1, g2, shf, scf, gf.reshape(1, D))


def _group_mods(mod, n, rows_per_group, per_row):
    outs = []
    for j in range(n):
        m = mod[:, j * D_MODEL:(j + 1) * D_MODEL]
        if per_row:
            outs.append(jnp.repeat(m, rows_per_group, axis=0)[None])
        else:
            outs.append(m[:, None, :])
    return outs


def kernel(x_prompt, x_sample, state_ret, cache_win_k, cache_win_v, c_prompt, c_sample,
           ada_w, ada_b, norm1_g, w_in, beta_ret, beta_att, w_out, norm2_g,
           router_w, router_b, w_up, b_up, w_down, b_down,
           final_ada_w, final_ada_b, final_norm_g):
    B, S, D = x_prompt.shape
    DB, L = x_sample.shape[:2]
    WB = cache_win_k.shape[2]
    Tp, Ts = B * S, DB * L
    assert ada_w.shape[0] == 1, "single-layer trunk"
    c_all = jnp.concatenate([c_prompt, c_sample], axis=0)
    c_all = jnp.pad(c_all, ((0, (-c_all.shape[0]) % 8), (0, 0)))
    mods = _adaln(c_all, ada_w[0], ada_b[0])
    fmods = _adaln(c_all, final_ada_w, final_ada_b)
    w_in_bf, w_out_bf = w_in[0].astype(BF16), w_out[0].astype(BF16)

    tmp = min(ROW_TILE, S)
    tms = Ts
    tmt = min(TOKEN_TILE, Ts)
    assert Tp % tmp == 0 and Tp % tms == 0 and Ts % tmt == 0 and Tp % tmt == 0
    mp = _group_mods(mods[:B], N_MOD, S, False)
    msm = _group_mods(mods[B:B + DB], N_MOD, L, True)
    fp = _group_mods(fmods[:B], 2, S, False)
    fs = _group_mods(fmods[B:B + DB], 2, L, True)

    cos_p, sin_p = _rope_tables(jnp.arange(S))
    outs = _inproj(x_prompt.reshape(Tp, D), mp[0], mp[1], norm1_g[0], w_in_bf, cos_p, sin_p,
                   tmp, S // tmp, True, B)
    qr, kr, vr, gr, ka, va = outs[:6]
    o_r_p, s_fin = _retention(qr, kr, vr, jnp.zeros((B, H_RET, DK_RET, DV_RET), F32), B, S, RET_CHUNK)
    o_a_p = _dilated_prompt(outs[6:], B, S)
    keep = min(MAX_WINDOW, S)
    win = lambda t: t.reshape(B, S, ATT_W)[:, S - keep:].reshape(B, keep, H_ATT, HD_ATT)
    wk_p, wv_p = win(ka), win(va)

    cos_s, sin_s = _rope_tables(PAST_LEN + jnp.arange(L))
    cos_s, sin_s = jnp.tile(cos_s, (DB, 1)), jnp.tile(sin_s, (DB, 1))
    qr_s, kr_s, vr_s, gr_s, qa_s, ka_s, va_s = _inproj(
        x_sample.reshape(Ts, D), msm[0], msm[1], norm1_g[0], w_in_bf, cos_s, sin_s, tms, 1, False)
    o_r_s, s_new = _retention(qr_s, kr_s, vr_s, state_ret[0], DB, L, L)
    o_a_s, nk, nv = _sample_attention(qa_s, ka_s, va_s, cache_win_k[0].reshape(DB, WB, ATT_W),
                                      cache_win_v[0].reshape(DB, WB, ATT_W), DB, L)

    T_all = Tp + Ts
    zero_cnt = jnp.zeros((1, N_EXPERTS), F32)
    x1_p, h2_p, ti_p, gt_p, rk_p, cnt_p = _merge(
        o_r_p, gr, o_a_p, x_prompt.reshape(Tp, D), mp[2], mp[3], mp[4], beta_ret[0], beta_att[0], w_out_bf,
        norm2_g[0], router_w[0], router_b[0], zero_cnt, tmp, S // tmp)
    x1_s, h2_s, ti_s, gt_s, rk_s, cnt = _merge(
        o_r_s, gr_s, o_a_s[None], x_sample.reshape(Ts, D), msm[2], msm[3], msm[4], beta_ret[0], beta_att[0],
        w_out_bf, norm2_g[0], router_w[0], router_b[0], cnt_p, tms, 1)

    top_i = jnp.concatenate([ti_p, ti_s], axis=0)
    rank = jnp.concatenate([rk_p, rk_s], axis=0)
    tme = min(MOE_TILE, T_all)
    dest, blk_exp, blk_new, n_used, fill_start, fill_n, nb = _route(top_i, rank, cnt, tme)
    fill_max = 1 << (tme - 1).bit_length() >> 1
    xs = _dispatch(h2_p, h2_s, dest, fill_start, fill_n, nb * tme, tmt, fill_max)
    yb = _moe_experts(xs, blk_exp, blk_new, n_used, w_up[0], b_up[0], w_down[0], b_down[0], tme)

    y_p = _combine(dest[:Tp], yb, gt_p, x1_p, mp[5], fp[0], fp[1], final_norm_g, tmt, S // tmt)
    y_s = _combine(dest[Tp:], yb, gt_s, x1_s, msm[5], fs[0], fs[1], final_norm_g, tmt, 1)

    shp = (DB, WB, H_ATT, HD_ATT)
    return (y_p.reshape(B, S, D), y_s.reshape(DB, L, D), s_fin[None], s_new[None],
            wk_p[None], wv_p[None], nk.reshape(shp)[None], nv.reshape(shp)[None])
```

```python
import functools
import math

import numpy as np
import jax
import jax.numpy as jnp
from jax import lax
from jax.experimental import pallas as pl
from jax.experimental.pallas import tpu as pltpu

F32 = jnp.float32
BF16 = jnp.bfloat16
HIGHEST = lax.Precision.HIGHEST

D_MODEL = 1024
PAST_LEN = 16384
H_RET, DK_RET, DV_RET = 4, 64, 128
RET_CHUNK = 256
H_ATT, HD_ATT = 8, 64
DILATED_PAIRS = ((128, 1), (512, 4), (2048, 16))
ATT_BLOCK = 128
MAX_WINDOW = 2048
ROPE_THETA = 10000.0
RET_QK = H_RET * DK_RET
RET_V = H_RET * DV_RET
ATT_W = H_ATT * HD_ATT
IN_WIDTH = 2 * RET_QK + 2 * RET_V + 3 * ATT_W
N_EXPERTS = 32
TOP_K = 4
D_FF = 1024
SWIGLU_LIMIT = 7.0
SWIGLU_ALPHA = 1.702
N_MOD = 6
EPS = 1e-6

LANES = 128
VMEM_LIMIT = 56 * 1024 * 1024
ROW_TILE = 512
MOE_TILE = 512
TOKEN_TILE = 256
N_SLABS = ATT_W // LANES
ROW_CHUNKS = D_MODEL // LANES
SEG = 8
ATT_SPAN = 2048

RET_LOG_DECAY = tuple(math.log(1.0 - 2.0 ** (-5.0 - h)) for h in range(H_RET))


def _params(sem, vmem=VMEM_LIMIT):
    return pltpu.CompilerParams(dimension_semantics=sem, vmem_limit_bytes=vmem)


def _sigmoid(z):
    return 1.0 / (1.0 + jnp.exp(-z))


def _adaln_body(c_ref, w_ref, b_ref, o_ref):
    c = c_ref[...]
    s = c * _sigmoid(c)
    o_ref[...] = jnp.dot(s, w_ref[...], precision=HIGHEST, preferred_element_type=F32) + b_ref[...]


def _adaln(c, w, b):
    R, D = c.shape
    N = w.shape[1]
    tn = 1024
    return pl.pallas_call(
        _adaln_body,
        grid=(N // tn,),
        in_specs=[pl.BlockSpec((R, D), lambda j: (0, 0)),
                  pl.BlockSpec((D, tn), lambda j: (0, j)),
                  pl.BlockSpec((1, tn), lambda j: (0, j))],
        out_specs=pl.BlockSpec((R, tn), lambda j: (0, j)),
        out_shape=jax.ShapeDtypeStruct((R, N), F32),
        compiler_params=_params(("arbitrary",)),
        name="adaln",
    )(c, w, b.reshape(1, N))


def _rope_chunk(xc, cos, sin_signed, first_half):
    partner = jnp.where(first_half, pltpu.roll(xc, 96, 1), pltpu.roll(xc, 32, 1))
    return xc * cos + partner * sin_signed


def _inproj_body(x_ref, sh_ref, sc_ref, g_ref, w_ref, cos_ref, sin_ref, *refs, dilated):
    x = x_ref[...]
    tm = x.shape[0]
    ms = jnp.mean(x * x, axis=-1, keepdims=True)
    h = x * lax.rsqrt(ms + EPS) * g_ref[...]
    h = h * (1.0 + sc_ref[...]) + sh_ref[...]
    p = jnp.dot(h.astype(BF16), w_ref[...], preferred_element_type=F32)
    cos = cos_ref[...]
    sin = sin_ref[...]
    first_half = (lax.broadcasted_iota(jnp.int32, (1, LANES), 1) % HD_ATT) < (HD_ATT // 2)

    def chunk(off, c, rope):
        xc = p[:, off + c * LANES: off + (c + 1) * LANES]
        return _rope_chunk(xc, cos, sin, first_half) if rope else xc

    qr_ref, kr_ref, vr_ref, gr_ref = refs[:4]
    o = 0
    for c in range(RET_QK // LANES):
        qr_ref[:, c * LANES:(c + 1) * LANES] = chunk(o, c, True)
    o += RET_QK
    for c in range(RET_QK // LANES):
        kr_ref[:, c * LANES:(c + 1) * LANES] = chunk(o, c, True) * (DK_RET ** -0.5)
    o += RET_QK
    vr_ref[...] = p[:, o:o + RET_V]; o += RET_V
    gr_ref[...] = p[:, o:o + RET_V]; o += RET_V
    if not dilated:
        qa_ref, ka_ref, va_ref = refs[4:7]
        for ref, rope in ((qa_ref, True), (ka_ref, True), (va_ref, False)):
            for c in range(N_SLABS):
                ref[:, c * LANES:(c + 1) * LANES] = chunk(o, c, rope)
            o += ATT_W
        return
    ka_ref, va_ref = refs[4:6]
    slab_refs = refs[6:15]
    scr, scr4 = refs[15:17]
    n4 = tm // 4
    n16 = tm // 16
    for t, (nat_ref, rope, scale) in enumerate(((None, True, HD_ATT ** -0.5), (ka_ref, True, 1.0),
                                                (va_ref, False, 1.0))):
        s1_ref, s4_ref, s16_ref = slab_refs[t], slab_refs[3 + t], slab_refs[6 + t]
        for c in range(N_SLABS):
            xc = chunk(o, c, rope)
            if nat_ref is not None:
                nat_ref[:, c * LANES:(c + 1) * LANES] = xc
            if scale != 1.0:
                xc = xc * scale
            scr[...] = xc
            s1_ref[c] = xc.astype(BF16)
            for r in range(4):
                part = scr[pl.ds(r, n4, stride=4), :]
                s4_ref[c, r] = part.astype(BF16)
                scr4[r] = part
            for r in range(4):
                for r2 in range(4):
                    s16_ref[c, 4 * r2 + r] = scr4[r, pl.ds(r2, n16, stride=4), :].astype(BF16)
        o += ATT_W


def _mod_spec(mod, tm, tiles_per_group):
    R = mod.shape[1]
    return pl.BlockSpec((None, R, mod.shape[2]), lambda i: (i // tiles_per_group, 0, 0))


def _inproj(x2d, sh, sc, g, w_bf, cos_t, sin_t, tm, tiles_per_group, dilated, B=1):
    T, D = x2d.shape
    pos_tiles = cos_t.shape[0] // tm
    row = lambda w: pl.BlockSpec((tm, w), lambda i: (i, 0))
    tab = pl.BlockSpec((tm, LANES), lambda i: (i % pos_tiles, 0))
    f32 = lambda w: jax.ShapeDtypeStruct((T, w), F32)
    if dilated:
        S = T // B
        tpb = S // tm
        widths = (RET_QK, RET_QK, RET_V, RET_V, ATT_W, ATT_W)
        out_specs = [row(w) for w in widths]
        out_shape = [f32(w) for w in widths]
        for dil in (1, 4, 16):
            for _ in range(3):
                if dil == 1:
                    out_specs.append(pl.BlockSpec((None, N_SLABS, tm, LANES),
                                                  lambda i: (i // tpb, 0, i % tpb, 0)))
                    out_shape.append(jax.ShapeDtypeStruct((B, N_SLABS, S, LANES), BF16))
                else:
                    out_specs.append(pl.BlockSpec((None, N_SLABS, dil, tm // dil, LANES),
                                                  lambda i: (i // tpb, 0, 0, i % tpb, 0)))
                    out_shape.append(jax.ShapeDtypeStruct((B, N_SLABS, dil, S // dil, LANES), BF16))
        scratch = [pltpu.VMEM((tm, LANES), F32), pltpu.VMEM((4, tm // 4, LANES), F32)]
    else:
        widths = (RET_QK, RET_QK, RET_V, RET_V, ATT_W, ATT_W, ATT_W)
        out_specs = [row(w) for w in widths]
        out_shape = [f32(w) for w in widths]
        scratch = []
    return pl.pallas_call(
        functools.partial(_inproj_body, dilated=dilated),
        grid=(T // tm,),
        in_specs=[row(D), _mod_spec(sh, tm, tiles_per_group), _mod_spec(sc, tm, tiles_per_group),
                  pl.BlockSpec((1, D), lambda i: (0, 0)),
                  pl.BlockSpec((D, IN_WIDTH), lambda i: (0, 0)), tab, tab],
        out_specs=out_specs, out_shape=out_shape, scratch_shapes=scratch,
        compiler_params=_params(("arbitrary",)),
        name="inproj",
    )(x2d, sh, sc, g.reshape(1, D), w_bf, cos_t, sin_t)


def _rope_tables(pos):
    half = HD_ATT // 2
    inv = ROPE_THETA ** (-jnp.arange(half, dtype=F32) / half)
    ang = pos.astype(F32)[:, None] * inv[None, :]
    cos = jnp.cos(ang)
    sin = jnp.sin(ang)
    cos_t = jnp.tile(cos, (1, LANES // half))
    sin_t = jnp.tile(jnp.concatenate([-sin, sin], axis=1), (1, LANES // HD_ATT))
    return cos_t, sin_t


def _retention_body(q_ref, k_ref, v_ref, s0_ref, o_ref, sfin_ref, s_scr, *, L):
    c = pl.program_id(1)

    @pl.when(c == 0)
    def _():
        s_scr[...] = s0_ref[...]

    ii = lax.broadcasted_iota(jnp.int32, (L, L), 0)
    jj = lax.broadcasted_iota(jnp.int32, (L, L), 1)
    diff = (ii - jj).astype(F32)
    causal = diff >= 0
    row = lax.broadcasted_iota(jnp.int32, (L, 1), 0).astype(F32)
    for h in range(H_RET):
        lg = RET_LOG_DECAY[h]
        decay = jnp.where(causal, jnp.exp(jnp.where(causal, diff, 0.0) * lg), 0.0)
        q = q_ref[:, h * DK_RET:(h + 1) * DK_RET]
        k = k_ref[:, h * DK_RET:(h + 1) * DK_RET]
        v = v_ref[:, h * DV_RET:(h + 1) * DV_RET]
        qb, vb = q.astype(BF16), v.astype(BF16)
        scores = lax.dot_general(qb, k.astype(BF16), (((1,), (1,)), ((), ())),
                                 preferred_element_type=F32) * decay
        S = s_scr[h]
        o = jnp.dot(scores.astype(BF16), vb, preferred_element_type=F32)
        o = o + jnp.dot(qb, S.astype(BF16), preferred_element_type=F32) * jnp.exp((row + 1.0) * lg)
        k_dec = k * jnp.exp((L - 1.0 - row) * lg)
        s_scr[h] = math.exp(L * lg) * S + lax.dot_general(
            k_dec.astype(BF16), vb, (((0,), (0,)), ((), ())), preferred_element_type=F32)
        o_ref[:, h * DV_RET:(h + 1) * DV_RET] = o

    @pl.when(c == pl.num_programs(1) - 1)
    def _():
        sfin_ref[...] = s_scr[...]


def _retention(q2d, k2d, v2d, s0, B, L_seq, L):
    nc = L_seq // L
    row = lambda w: pl.BlockSpec((L, w), lambda b, c: (b * nc + c, 0))
    st = pl.BlockSpec((None, H_RET, DK_RET, DV_RET), lambda b, c: (b, 0, 0, 0))
    return pl.pallas_call(
        functools.partial(_retention_body, L=L),
        grid=(B, nc),
        in_specs=[row(RET_QK), row(RET_QK), row(RET_V), st],
        out_specs=[row(RET_V), st],
        out_shape=[jax.ShapeDtypeStruct((B * L_seq, RET_V), F32),
                   jax.ShapeDtypeStruct((B, H_RET, DK_RET, DV_RET), F32)],
        scratch_shapes=[pltpu.VMEM((H_RET, DK_RET, DV_RET), F32)],
        compiler_params=_params(("arbitrary", "arbitrary")),
        name="retention",
    )(q2d, k2d, v2d, s0)


def _band_attention(q, kcat, vcat, valid, low):
    nt = (((1,), (1,)), ((), ()))
    zero = jnp.zeros_like(q)
    outs = []
    for hh in range(2):
        qh = jnp.where(low if hh == 0 else ~low, q, zero)
        s = lax.dot_general(qh, kcat, nt, preferred_element_type=F32)
        s = jnp.where(valid, s, -jnp.inf)
        m = jnp.max(s, axis=-1, keepdims=True)
        p = jnp.exp(s - m)
        l = jnp.sum(p, axis=-1, keepdims=True)
        outs.append((jnp.dot(p.astype(BF16), vcat, preferred_element_type=F32), m, l))
    (a0, m0, l0), (a1, m1, l1) = outs
    shape = a0.shape
    return (jnp.where(low, a0, a1), jnp.where(low, jnp.broadcast_to(m0, shape), jnp.broadcast_to(m1, shape)),
            jnp.where(low, jnp.broadcast_to(l0, shape), jnp.broadcast_to(l1, shape)))


def _dilated_body(q1, k1, k1p, v1, v1p, q4, k4, k4p, v4, v4p, q16, k16, k16p, v16, v16p,
                  o_ref, kf1, vf1, kf4, vf4, acc_s, m_s, l_s):
    j = pl.program_id(2)
    nq = ATT_BLOCK
    n1 = ATT_SPAN // nq
    GROUP = 4
    kf1[0:nq] = k1p[...]
    kf1[nq:] = k1[...]
    vf1[0:nq] = v1p[...]
    vf1[nq:] = v1[...]
    for r in range(4):
        kf4[r, 0:nq] = k4p[r]
        kf4[r, nq:] = k4[r]
        vf4[r, 0:nq] = v4p[r]
        vf4[r, nq:] = v4[r]
    qi = lax.broadcasted_iota(jnp.int32, (nq, 2 * nq), 0) + nq
    ki = lax.broadcasted_iota(jnp.int32, (nq, 2 * nq), 1)
    dist = qi - ki
    band = (dist >= 0) & (dist <= nq)
    in_cur = ki >= nq
    low = lax.broadcasted_iota(jnp.int32, (1, LANES), 1) < HD_ATT

    def mask(has_prev):
        return band & (has_prev | in_cur)

    def merge(rows, a, m, l):
        m_old, l_old = m_s[rows, :], l_s[rows, :]
        m_new = jnp.maximum(m_old, m)
        w_old, w_new = jnp.exp(m_old - m_new), jnp.exp(m - m_new)
        return acc_s[rows, :] * w_old + a * w_new, m_new, l_old * w_old + l * w_new

    def dil1(g, carry):
        res = []
        for u in range(GROUP):
            qb = g * GROUP + u
            r0 = pl.multiple_of(qb * nq, nq)
            res.append((pl.ds(r0, nq), _band_attention(
                q1[pl.ds(r0, nq), :], kf1[pl.ds(r0, 2 * nq), :], vf1[pl.ds(r0, 2 * nq), :],
                mask((j > 0) | (qb > 0)), low)))
        for rows, (a, m, l) in res:
            acc_s[rows, :] = a
            m_s[rows, :] = m
            l_s[rows, :] = l
        return carry

    lax.fori_loop(0, n1 // GROUP, dil1, 0)

    def dil4(qb, carry):
        r0 = pl.multiple_of(qb * nq, nq)
        res = []
        for r in range(4):
            res.append((pl.ds(qb * (4 * nq) + r, nq, stride=4), _band_attention(
                q4[r, pl.ds(r0, nq), :], kf4[r, pl.ds(r0, 2 * nq), :], vf4[r, pl.ds(r0, 2 * nq), :],
                mask((j > 0) | (qb > 0)), low)))
        for rows, (a, m, l) in res:
            a, m, l = merge(rows, a, m, l)
            acc_s[rows, :] = a
            m_s[rows, :] = m
            l_s[rows, :] = l
        return carry

    lax.fori_loop(0, ATT_SPAN // 4 // nq, dil4, 0)

    def dil16(g, carry):
        res = []
        for u in range(GROUP):
            r = g * GROUP + u
            kcat = jnp.concatenate([k16p[r], k16[r]], axis=0)
            vcat = jnp.concatenate([v16p[r], v16[r]], axis=0)
            res.append((pl.ds(r, nq, stride=16), _band_attention(q16[r], kcat, vcat, mask(j > 0), low)))
        for rows, (a, m, l) in res:
            a, m, l = merge(rows, a, m, l)
            o_ref[rows, :] = a / l
        return carry

    lax.fori_loop(0, 16 // GROUP, dil16, 0)


def _dilated_prompt(slabs, B, S):
    q1, k1, v1, q4, k4, v4, q16, k16, v16 = slabs
    assert S % ATT_SPAN == 0 and DILATED_PAIRS == ((128, 1), (512, 4), (2048, 16))
    nq = ATT_BLOCK
    prev = lambda j, per: jnp.maximum(j * per - 1, 0)
    s1 = pl.BlockSpec((None, None, ATT_SPAN, LANES), lambda b, c, j: (b, c, j, 0))
    s1p = pl.BlockSpec((None, None, nq, LANES), lambda b, c, j: (b, c, prev(j, ATT_SPAN // nq), 0))
    s4 = pl.BlockSpec((None, None, 4, ATT_SPAN // 4, LANES), lambda b, c, j: (b, c, 0, j, 0))
    s4p = pl.BlockSpec((None, None, 4, nq, LANES), lambda b, c, j: (b, c, 0, prev(j, ATT_SPAN // 4 // nq), 0))
    s16 = pl.BlockSpec((None, None, 16, nq, LANES), lambda b, c, j: (b, c, 0, j, 0))
    s16p = pl.BlockSpec((None, None, 16, nq, LANES), lambda b, c, j: (b, c, 0, prev(j, 1), 0))
    return pl.pallas_call(
        _dilated_body,
        grid=(B, N_SLABS, S // ATT_SPAN),
        in_specs=[s1, s1, s1p, s1, s1p, s4, s4, s4p, s4, s4p, s16, s16, s16p, s16, s16p],
        out_specs=pl.BlockSpec((None, None, ATT_SPAN, LANES), lambda b, c, j: (b, c, j, 0)),
        out_shape=jax.ShapeDtypeStruct((B, N_SLABS, S, LANES), F32),
        scratch_shapes=[pltpu.VMEM((ATT_SPAN + nq, LANES), BF16), pltpu.VMEM((ATT_SPAN + nq, LANES), BF16),
                        pltpu.VMEM((4, ATT_SPAN // 4 + nq, LANES), BF16),
                        pltpu.VMEM((4, ATT_SPAN // 4 + nq, LANES), BF16),
                        pltpu.VMEM((ATT_SPAN, LANES), F32), pltpu.VMEM((ATT_SPAN, LANES), F32),
                        pltpu.VMEM((ATT_SPAN, LANES), F32)],
        compiler_params=_params(("arbitrary", "arbitrary", "arbitrary")),
        name="dilated_attention",
    )(q1, k1, k1, v1, v1, q4, k4, k4, v4, v4, q16, k16, k16, v16, v16)


def _sample_multiplicity(L, WB):
    pad = LANES
    cnt = np.zeros((L, WB + pad), np.float32)
    for l in range(L):
        for window, dil in DILATED_PAIRS:
            for j in range(window // dil + 1):
                idx = WB + l - dil * j
                if idx >= 0:
                    cnt[l, idx] += 1.0
    cnt = np.tile(cnt, (H_ATT, 1))
    return cnt[:, :WB], cnt[:, WB:]


def _sample_att_body(q_ref, kn_ref, vn_ref, bk_ref, bv_ref, cb_ref, cn_ref,
                     o_ref, nk_ref, nv_ref, *, L, WB):
    q = q_ref[...]
    lane_head = lax.broadcasted_iota(jnp.int32, (1, ATT_W), 1) // HD_ATT
    qexp = jnp.concatenate([jnp.where(lane_head == h, q, 0.0) for h in range(H_ATT)], axis=0).astype(BF16)
    zpad = jnp.zeros((LANES - L, ATT_W), F32)
    kn = jnp.concatenate([kn_ref[...], zpad], axis=0).astype(BF16)
    vn = jnp.concatenate([vn_ref[...], zpad], axis=0).astype(BF16)
    bk = bk_ref[...]
    bv = bv_ref[...]
    nt = (((1,), (1,)), ((), ()))
    scale = HD_ATT ** -0.5
    sb = lax.dot_general(qexp, bk.astype(BF16), nt, preferred_element_type=F32) * scale
    sn = lax.dot_general(qexp, kn, nt, preferred_element_type=F32) * scale
    cb = cb_ref[...]
    cn = cn_ref[...]
    sb = jnp.where(cb > 0, sb, -jnp.inf)
    sn = jnp.where(cn > 0, sn, -jnp.inf)
    m = jnp.maximum(jnp.max(sb, axis=-1, keepdims=True), jnp.max(sn, axis=-1, keepdims=True))
    pb = cb * jnp.exp(sb - m)
    pn = cn * jnp.exp(sn - m)
    l = jnp.sum(pb, axis=-1, keepdims=True) + jnp.sum(pn, axis=-1, keepdims=True)
    o_all = (jnp.dot(pb.astype(BF16), bv.astype(BF16), preferred_element_type=F32)
             + jnp.dot(pn.astype(BF16), vn, preferred_element_type=F32)) / l
    o = jnp.zeros((L, ATT_W), F32)
    for h in range(H_ATT):
        o = o + jnp.where(lane_head == h, o_all[h * L:(h + 1) * L], 0.0)
    for c in range(N_SLABS):
        o_ref[c] = o[:, c * LANES:(c + 1) * LANES]
    nk_ref[0:WB - L, :] = bk[L:WB]
    nk_ref[WB - L:WB, :] = kn_ref[...]
    nv_ref[0:WB - L, :] = bv[L:WB]
    nv_ref[WB - L:WB, :] = vn_ref[...]


def _sample_attention(q2d, k2d, v2d, buf_k, buf_v, DB, L):
    WB = buf_k.shape[1]
    cb, cn = _sample_multiplicity(L, WB)
    row = pl.BlockSpec((L, ATT_W), lambda b: (b, 0))
    buf = pl.BlockSpec((None, WB, ATT_W), lambda b: (b, 0, 0))
    const = lambda a: pl.BlockSpec(a.shape, lambda b: (0, 0))
    return pl.pallas_call(
        functools.partial(_sample_att_body, L=L, WB=WB),
        grid=(DB,),
        in_specs=[row, row, row, buf, buf, const(cb), const(cn)],
        out_specs=[pl.BlockSpec((N_SLABS, L, LANES), lambda b: (0, b, 0)), buf, buf],
        out_shape=[jax.ShapeDtypeStruct((N_SLABS, DB * L, LANES), F32),
                   jax.ShapeDtypeStruct((DB, WB, ATT_W), F32),
                   jax.ShapeDtypeStruct((DB, WB, ATT_W), F32)],
        compiler_params=_params(("arbitrary",)),
        name="sample_attention",
    )(q2d, k2d, v2d, buf_k, buf_v, jnp.asarray(cb), jnp.asarray(cn))


def _group_mean(sq, width):
    if width == LANES:
        return jnp.mean(sq, axis=-1, keepdims=True)
    lane = lax.broadcasted_iota(jnp.int32, (1, LANES), 1)
    low = lane < width
    s_lo = jnp.sum(jnp.where(low, sq, 0.0), axis=-1, keepdims=True)
    s_hi = jnp.sum(jnp.where(low, 0.0, sq), axis=-1, keepdims=True)
    return jnp.where(low, s_lo, s_hi) * (1.0 / width)


def _merge_body(or_ref, gr_ref, oa0, oa1, oa2, oa3, x_ref, g1_ref, sh_ref, sc_ref, br_ref, ba_ref, wo_ref,
                n2_ref, rw_ref, rb_ref, cin_ref,
                x1_ref, h2_ref, ti_ref, gt_ref, rk_ref, p8_ref, meta_ref, cnt_ref, cnt_scr):
    i = pl.program_id(0)

    @pl.when(i == 0)
    def _():
        cnt_scr[...] = cin_ref[...]

    parts = []
    for c in range(RET_V // LANES):
        cs = slice(c * LANES, (c + 1) * LANES)
        o = or_ref[:, cs]
        o = o * lax.rsqrt(_group_mean(o * o, DV_RET) + EPS) * br_ref[:, cs]
        g = gr_ref[:, cs]
        parts.append((o * (g * _sigmoid(g))).astype(BF16))
    for c, oa in enumerate((oa0, oa1, oa2, oa3)):
        cs = slice(c * LANES, (c + 1) * LANES)
        o = oa[...]
        o = o * lax.rsqrt(_group_mean(o * o, HD_ATT) + EPS) * ba_ref[:, cs]
        parts.append(o.astype(BF16))
    mix = jnp.concatenate(parts, axis=1)
    y = jnp.dot(mix, wo_ref[...], preferred_element_type=F32)
    x1 = x_ref[...] + g1_ref[...] * y
    x1_ref[...] = x1
    ms = jnp.mean(x1 * x1, axis=-1, keepdims=True)
    h2 = x1 * lax.rsqrt(ms + EPS) * n2_ref[...]
    h2 = h2 * (1.0 + sc_ref[...]) + sh_ref[...]
    h2_ref[...] = h2
    rw = rw_ref[...]
    h_hi = h2.astype(BF16)
    h_lo = (h2 - h_hi.astype(F32)).astype(BF16)
    w_hi = rw.astype(BF16)
    w_lo = (rw - w_hi.astype(F32)).astype(BF16)
    logits = (jnp.dot(h_hi, w_hi, preferred_element_type=F32) + jnp.dot(h_lo, w_hi, preferred_element_type=F32)
              + jnp.dot(h_hi, w_lo, preferred_element_type=F32)) + rb_ref[...]
    tm = logits.shape[0]
    lane = lax.broadcasted_iota(jnp.int32, logits.shape, 1).astype(F32)
    vals, idxs = [], []
    for _ in range(TOP_K):
        m = jnp.max(logits, axis=-1, keepdims=True)
        idx = jnp.min(jnp.where(logits == m, lane, float(N_EXPERTS)), axis=-1, keepdims=True)
        vals.append(m)
        idxs.append(idx)
        logits = jnp.where(lane == idx, -jnp.inf, logits)
    e = [jnp.exp(v - vals[0]) for v in vals]
    tot = e[0] + e[1] + e[2] + e[3]
    for k in range(TOP_K):
        ti_ref[:, k:k + 1] = idxs[k].astype(jnp.int32)
        gt_ref[:, k:k + 1] = e[k] / tot
    member = jnp.zeros(logits.shape, F32)
    for k in range(TOP_K):
        member = member + jnp.where(lane == idxs[k], 1.0, 0.0)
    sub = min(TOKEN_TILE, tm)
    ti = lax.broadcasted_iota(jnp.int32, (tm, tm), 0)
    tj = lax.broadcasted_iota(jnp.int32, (tm, tm), 1)
    before = jnp.where((tj < ti) & (tj // sub == ti // sub), 1.0, 0.0).astype(BF16)
    rin = jnp.dot(before, member.astype(BF16), preferred_element_type=F32)
    earlier = jnp.where(lax.broadcasted_iota(jnp.int32, (N_EXPERTS, N_EXPERTS), 0)
                        < lax.broadcasted_iota(jnp.int32, (N_EXPERTS, N_EXPERTS), 1), 1.0, 0.0)
    row_sub = lax.broadcasted_iota(jnp.int32, (tm, 1), 0) // sub
    carry = cnt_scr[...]
    base = jnp.zeros(logits.shape, F32)
    off8 = jnp.zeros(logits.shape, F32)
    for s in range(tm // sub):
        in_sub = row_sub == s
        n = jnp.sum(jnp.where(in_sub, member, 0.0), axis=0, keepdims=True)
        n8 = jnp.floor((n + 7.0) * 0.125) * 8.0
        o8 = jnp.dot(jnp.broadcast_to(n8, (8, N_EXPERTS)), earlier, precision=HIGHEST,
                     preferred_element_type=F32)
        o8 = jnp.max(o8, axis=0, keepdims=True)
        base = base + jnp.where(in_sub, carry, 0.0)
        off8 = off8 + jnp.where(in_sub, o8, 0.0)
        meta_ref[s, 0:1, :] = n
        meta_ref[s, 1:2, :] = carry
        carry = carry + n
    for k in range(TOP_K):
        sel = lane == idxs[k]
        rk_ref[:, k:k + 1] = jnp.sum(jnp.where(sel, rin + base, 0.0), axis=-1,
                                     keepdims=True).astype(jnp.int32)
        p8_ref[:, k:k + 1] = jnp.sum(jnp.where(sel, rin + off8, 0.0), axis=-1,
                                     keepdims=True).astype(jnp.int32)
    cnt_scr[...] = carry
    cnt_ref[...] = carry


def _merge(o_r, g_r, o_a, x2d, g1, sh2, sc2, beta_ret, beta_att, wo_bf, n2, rw, rb, cnt_in, tm,
           tiles_per_group):
    T, D = x2d.shape
    tpg = o_a.shape[2] // tm
    sub = min(TOKEN_TILE, tm)
    row = lambda w: pl.BlockSpec((tm, w), lambda i: (i, 0))
    const = lambda r, c: pl.BlockSpec((r, c), lambda i: (0, 0))
    ms = lambda m: _mod_spec(m, tm, tiles_per_group)
    slab = lambda c: pl.BlockSpec((None, None, tm, LANES), lambda i: (i // tpg, c, i % tpg, 0))
    ins = [o_r, g_r, o_a, o_a, o_a, o_a, x2d, g1, sh2, sc2, beta_ret.reshape(1, RET_V),
           beta_att.reshape(1, ATT_W), wo_bf, n2.reshape(1, D), rw, rb.reshape(1, N_EXPERTS), cnt_in]
    in_specs = [row(RET_V), row(RET_V), slab(0), slab(1), slab(2), slab(3), row(D), ms(g1), ms(sh2), ms(sc2),
                const(1, RET_V), const(1, ATT_W), const(D, D), const(1, D),
                const(D, N_EXPERTS), const(1, N_EXPERTS), const(1, N_EXPERTS)]
    return pl.pallas_call(
        _merge_body,
        grid=(T // tm,),
        in_specs=in_specs,
        out_specs=[row(D), row(D), row(TOP_K), row(TOP_K), row(TOP_K), row(TOP_K),
                   pl.BlockSpec((tm // sub, 2, N_EXPERTS), lambda i: (i, 0, 0)), const(1, N_EXPERTS)],
        out_shape=[jax.ShapeDtypeStruct((T, D), F32), jax.ShapeDtypeStruct((T, D), F32),
                   jax.ShapeDtypeStruct((T, TOP_K), jnp.int32), jax.ShapeDtypeStruct((T, TOP_K), F32),
                   jax.ShapeDtypeStruct((T, TOP_K), jnp.int32), jax.ShapeDtypeStruct((T, TOP_K), jnp.int32),
                   jax.ShapeDtypeStruct((T // sub, 2, N_EXPERTS), F32),
                   jax.ShapeDtypeStruct((1, N_EXPERTS), F32)],
        scratch_shapes=[pltpu.VMEM((1, N_EXPERTS), F32)],
        compiler_params=_params(("arbitrary",)),
        name="merge_router",
    )(*ins)


def _route(top_i, rank, counts, tm):
    T = top_i.shape[0]
    A = T * TOP_K
    nb = -(-(A + N_EXPERTS * (tm - 1)) // tm)
    counts = counts.reshape(N_EXPERTS).astype(jnp.int32)
    padded = (counts + tm - 1) // tm * tm
    pad_end = jnp.cumsum(padded)
    pad_start = pad_end - padded
    onehot = top_i[:, :, None] == jnp.arange(N_EXPERTS, dtype=jnp.int32)[None, None, :]
    dest = rank + jnp.sum(jnp.where(onehot, pad_start[None, None, :], 0), axis=-1)
    blk0 = jnp.arange(nb + 1, dtype=jnp.int32) * tm
    blk_exp = jnp.minimum(jnp.sum((pad_end[None, :] <= blk0[:, None]).astype(jnp.int32), axis=1),
                          N_EXPERTS - 1).astype(jnp.int32)
    n_used = (pad_end[-1:] // tm).astype(jnp.int32)
    blk_new = jnp.concatenate([jnp.ones((1,), jnp.int32),
                               (blk_exp[1:] != blk_exp[:-1]).astype(jnp.int32)])
    fill_start = jnp.concatenate([pad_start + counts, pad_end[-1:]]).astype(jnp.int32)
    fill_n = jnp.concatenate([padded - counts, nb * tm - pad_end[-1:]]).astype(jnp.int32)
    return dest.astype(jnp.int32), blk_exp, blk_new, n_used, fill_start, fill_n, pad_start, nb


def _dispatch_body(fs_ref, fn_ref, dst_ref, hp_ref, hs_ref, xs_hbm, zbuf, sem, zsem, *, tm, fill_max,
                   prompt_tiles):
    i = pl.program_id(0)

    def scatter(h_ref):
        def issue(r, carry):
            for k in range(TOP_K):
                pltpu.make_async_copy(h_ref.at[pl.ds(r, 1)], xs_hbm.at[pl.ds(dst_ref[r * TOP_K + k], 1)],
                                      sem).start()
            return carry
        lax.fori_loop(0, tm, issue, 0, unroll=32)

    @pl.when(i < prompt_tiles)
    def _():
        scatter(hp_ref)

    @pl.when(i >= prompt_tiles)
    def _():
        scatter(hs_ref)

    @pl.when(i == pl.num_programs(0) - 1)
    def _():
        zbuf[...] = jnp.zeros(zbuf.shape, F32)

        def fill(e, wait):
            start = fs_ref[e]
            n = fn_ref[e]
            head = jnp.minimum((-start) & 7, n)

            def one(r, carry):
                cp = pltpu.make_async_copy(zbuf.at[pl.ds(0, 1)], xs_hbm.at[pl.ds(start + r, 1)], zsem)
                cp.wait() if wait else cp.start()
                return carry

            lax.fori_loop(0, head, one, 0)
            start = start + head
            n = n - head
            sz = fill_max
            while sz >= 8:
                @pl.when((n & sz) != 0)
                def _(start=start, sz=sz):
                    cp = pltpu.make_async_copy(zbuf.at[pl.ds(0, sz)],
                                               xs_hbm.at[pl.ds(pl.multiple_of(start, 8), sz)], zsem)
                    cp.wait() if wait else cp.start()
                start = start + (n & sz)
                sz //= 2

        def fill_all(wait):
            def body(e, carry):
                fill(e, wait)
                return carry
            lax.fori_loop(0, N_EXPERTS, body, 0)

            def tail(c, carry):
                row = pl.multiple_of(fs_ref[N_EXPERTS] + c * fill_max, fill_max)
                cp = pltpu.make_async_copy(zbuf, xs_hbm.at[pl.ds(row, fill_max)], zsem)
                cp.wait() if wait else cp.start()
                return carry
            lax.fori_loop(0, fn_ref[N_EXPERTS] // fill_max, tail, 0)

        fill_all(False)
        fill_all(True)

    for k in range(TOP_K):
        pltpu.make_async_copy(hp_ref, xs_hbm.at[pl.ds(0, tm)], sem).wait()


def _dispatch(h2_p, h2_s, dest, fill_start, fill_n, R, tm, fill_max):
    Tp, D = h2_p.shape
    Ts = h2_s.shape[0]
    np_, ns_ = Tp // tm, Ts // tm
    grid_spec = pltpu.PrefetchScalarGridSpec(
        num_scalar_prefetch=2,
        grid=(np_ + ns_,),
        in_specs=[pl.BlockSpec((tm * TOP_K,), lambda i, fs, fn: (i,), memory_space=pltpu.SMEM),
                  pl.BlockSpec((tm, D), lambda i, fs, fn: (jnp.minimum(i, np_ - 1), 0)),
                  pl.BlockSpec((tm, D), lambda i, fs, fn: (jnp.maximum(i - np_, 0), 0))],
        out_specs=pl.BlockSpec(memory_space=pl.ANY),
        scratch_shapes=[pltpu.VMEM((fill_max, D), F32), pltpu.SemaphoreType.DMA(()),
                        pltpu.SemaphoreType.DMA(())],
    )
    return pl.pallas_call(
        functools.partial(_dispatch_body, tm=tm, fill_max=fill_max, prompt_tiles=np_),
        grid_spec=grid_spec,
        out_shape=jax.ShapeDtypeStruct((R, D), F32),
        compiler_params=_params(("arbitrary",)),
        name="moe_dispatch",
    )(fill_start, fill_n, dest.reshape(-1), h2_p, h2_s)


def _moe_body(nused_ref, bexp_ref, new_ref, x_ref, wu_ref, bu_ref, wd_ref, bd_ref, y_ref, wu_bf, wd_bf):
    i = pl.program_id(0)

    @pl.when(new_ref[i] != 0)
    def _():
        wu_bf[...] = wu_ref[...].astype(BF16)
        wd_bf[...] = wd_ref[...].astype(BF16)

    @pl.when(i < nused_ref[0])
    def _():
        x = x_ref[...].astype(BF16)
        u = jnp.dot(x, wu_bf[...], preferred_element_type=F32) + bu_ref[...]
        x_glu = jnp.minimum(u[:, :D_FF], SWIGLU_LIMIT)
        x_lin = jnp.clip(u[:, D_FF:], -SWIGLU_LIMIT, SWIGLU_LIMIT)
        act = x_glu * _sigmoid(SWIGLU_ALPHA * x_glu) * (x_lin + 1.0)
        y = jnp.dot(act.astype(BF16), wd_bf[...], preferred_element_type=F32) + bd_ref[...]
        tm = y.shape[0]
        for c in range(ROW_CHUNKS):
            y_ref[pl.ds(c, tm, stride=ROW_CHUNKS), :] = y[:, c * LANES:(c + 1) * LANES]

    @pl.when(i >= nused_ref[0])
    def _():
        y_ref[...] = jnp.zeros(y_ref.shape, F32)


def _moe_experts(xs, blk_exp, blk_new, n_used, w_up, bu, w_down, bd, tm):
    R, D = xs.shape
    blk = lambda i, nu, be, nw: (jnp.minimum(i, nu[0] - 1), 0)
    wspec = lambda r, c: pl.BlockSpec((None, r, c), lambda i, nu, be, nw: (be[i], 0, 0))
    grid_spec = pltpu.PrefetchScalarGridSpec(
        num_scalar_prefetch=3,
        grid=(R // tm + 1,),
        in_specs=[pl.BlockSpec((tm, D), blk), wspec(D, 2 * D_FF), wspec(1, 2 * D_FF),
                  wspec(D_FF, D), wspec(1, D)],
        out_specs=pl.BlockSpec((tm * ROW_CHUNKS, LANES), lambda i, nu, be, nw: (i, 0)),
        scratch_shapes=[pltpu.VMEM((D, 2 * D_FF), BF16), pltpu.VMEM((D_FF, D), BF16)],
    )
    return pl.pallas_call(
        _moe_body,
        grid_spec=grid_spec,
        out_shape=jax.ShapeDtypeStruct(((R + tm) * ROW_CHUNKS, LANES), F32),
        compiler_params=_params(("arbitrary",)),
        name="moe_experts",
    )(n_used, blk_exp, blk_new, xs, w_up, bu.reshape(N_EXPERTS, 1, 2 * D_FF), w_down,
      bd.reshape(N_EXPERTS, 1, D))


def _combine_body(st_ref, n_ref, tot_ref, yb_hbm, p8_ref, gate_ref, x1_ref, g2_ref, shf_ref, scf_ref, gf_ref,
                  out_ref, sbuf, sem, *, tm, slots):
    i = pl.program_id(0)
    nsteps = pl.num_programs(0)
    seg_rows = SEG * ROW_CHUNKS

    @pl.when(i == 0)
    def _():
        sbuf[...] = jnp.zeros(sbuf.shape, F32)

    def start_runs(tile, slot):
        def per_expert(e, off):
            n = n_ref[tile * N_EXPERTS + e]
            st = st_ref[tile * N_EXPERTS + e]
            nseg = (n + SEG - 1) // SEG

            def one(j, carry):
                src = pl.multiple_of((st + j * SEG) * ROW_CHUNKS, ROW_CHUNKS)
                dst = pl.multiple_of((off + j * SEG) * ROW_CHUNKS, seg_rows)
                pltpu.make_async_copy(yb_hbm.at[pl.ds(src, seg_rows)],
                                      sbuf.at[slot, pl.ds(dst, seg_rows)], sem.at[slot]).start()
                return carry

            lax.fori_loop(0, nseg, one, 0)
            return off + nseg * SEG

        lax.fori_loop(0, N_EXPERTS, per_expert, 0)

    def wait_runs(tile, slot):
        nseg = tot_ref[tile]
        sz = 1 << ((slots // SEG).bit_length() - 1)
        while sz >= 1:
            @pl.when((nseg & sz) != 0)
            def _(sz=sz):
                pltpu.make_async_copy(yb_hbm.at[pl.ds(0, sz * seg_rows)],
                                      sbuf.at[slot, pl.ds(0, sz * seg_rows)], sem.at[slot]).wait()
            sz //= 2

    @pl.when(i == 0)
    def _():
        start_runs(0, 0)

    @pl.when(i + 1 < nsteps)
    def _():
        start_runs(i + 1, (i + 1) % 2)

    slot = i % 2
    wait_runs(i, slot)
    pos = p8_ref[...]
    gates = gate_ref[...]
    s_iota = lax.broadcasted_iota(jnp.int32, (tm, slots), 1)
    g = jnp.zeros((tm, slots), F32)
    for k in range(TOP_K):
        g = g + jnp.where(s_iota == pos[:, k:k + 1], gates[:, k:k + 1], 0.0)
    gb = g.astype(BF16)
    ys = []
    for c in range(ROW_CHUNKS):
        chunk = sbuf[slot, pl.ds(c, slots, stride=ROW_CHUNKS), :]
        ys.append(jnp.dot(gb, chunk.astype(BF16), preferred_element_type=F32))
    y = jnp.concatenate(ys, axis=1)
    x2 = x1_ref[...] + g2_ref[...] * y
    ms = jnp.mean(x2 * x2, axis=-1, keepdims=True)
    h = x2 * lax.rsqrt(ms + EPS) * gf_ref[...]
    out_ref[...] = h * (1.0 + scf_ref[...]) + shf_ref[...]


def _combine(seg_start, seg_n, seg_tot, yb, pos8, gates, x1, g2, shf, scf, gf, tm, tiles_per_group):
    T, D = x1.shape
    nsteps = T // tm
    slots = -(-(tm * TOP_K + N_EXPERTS * SEG) // LANES) * LANES
    row = lambda w: pl.BlockSpec((tm, w), lambda i, st, n, tot: (i, 0))
    ms = lambda m: pl.BlockSpec((None, m.shape[1], m.shape[2]),
                                lambda i, st, n, tot: (i // tiles_per_group, 0, 0))
    grid_spec = pltpu.PrefetchScalarGridSpec(
        num_scalar_prefetch=3,
        grid=(nsteps,),
        in_specs=[pl.BlockSpec(memory_space=pl.ANY), row(TOP_K), row(TOP_K), row(D), ms(g2), ms(shf), ms(scf),
                  pl.BlockSpec((1, D), lambda i, st, n, tot: (0, 0))],
        out_specs=row(D),
        scratch_shapes=[pltpu.VMEM((2, slots * ROW_CHUNKS, LANES), F32), pltpu.SemaphoreType.DMA((2,))],
    )
    return pl.pallas_call(
        functools.partial(_combine_body, tm=tm, slots=slots),
        grid_spec=grid_spec,
        out_shape=jax.ShapeDtypeStruct((T, D), F32),
        compiler_params=_params(("arbitrary",)),
        name="combine_final",
    )(seg_start, seg_n, seg_tot, yb, pos8, gates, x1, g2, shf, scf, gf.reshape(1, D))


def _group_mods(mod, n, rows_per_group, per_row):
    outs = []
    for j in range(n):
        m = mod[:, j * D_MODEL:(j + 1) * D_MODEL]
        if per_row:
            outs.append(jnp.repeat(m, rows_per_group, axis=0)[None])
        else:
            outs.append(m[:, None, :])
    return outs


def kernel(x_prompt, x_sample, state_ret, cache_win_k, cache_win_v, c_prompt, c_sample,
           ada_w, ada_b, norm1_g, w_in, beta_ret, beta_att, w_out, norm2_g,
           router_w, router_b, w_up, b_up, w_down, b_down,
           final_ada_w, final_ada_b, final_norm_g):
    B, S, D = x_prompt.shape
    DB, L = x_sample.shape[:2]
    WB = cache_win_k.shape[2]
    Tp, Ts = B * S, DB * L
    assert ada_w.shape[0] == 1, "single-layer trunk"
    c_all = jnp.concatenate([c_prompt, c_sample], axis=0)
    c_all = jnp.pad(c_all, ((0, (-c_all.shape[0]) % 8), (0, 0)))
    mods = _adaln(c_all, ada_w[0], ada_b[0])
    fmods = _adaln(c_all, final_ada_w, final_ada_b)
    w_in_bf, w_out_bf = w_in[0].astype(BF16), w_out[0].astype(BF16)

    tmp = min(ROW_TILE, S)
    tms = Ts
    tmt = min(TOKEN_TILE, Ts)
    assert Tp % tmp == 0 and Tp % tms == 0 and Ts % tmt == 0 and Tp % tmt == 0
    mp = _group_mods(mods[:B], N_MOD, S, False)
    msm = _group_mods(mods[B:B + DB], N_MOD, L, True)
    fp = _group_mods(fmods[:B], 2, S, False)
    fs = _group_mods(fmods[B:B + DB], 2, L, True)

    cos_p, sin_p = _rope_tables(jnp.arange(S))
    outs = _inproj(x_prompt.reshape(Tp, D), mp[0], mp[1], norm1_g[0], w_in_bf, cos_p, sin_p,
                   tmp, S // tmp, True, B)
    qr, kr, vr, gr, ka, va = outs[:6]
    o_r_p, s_fin = _retention(qr, kr, vr, jnp.zeros((B, H_RET, DK_RET, DV_RET), F32), B, S, RET_CHUNK)
    o_a_p = _dilated_prompt(outs[6:], B, S)
    keep = min(MAX_WINDOW, S)
    win = lambda t: t.reshape(B, S, ATT_W)[:, S - keep:].reshape(B, keep, H_ATT, HD_ATT)
    wk_p, wv_p = win(ka), win(va)

    cos_s, sin_s = _rope_tables(PAST_LEN + jnp.arange(L))
    cos_s, sin_s = jnp.tile(cos_s, (DB, 1)), jnp.tile(sin_s, (DB, 1))
    qr_s, kr_s, vr_s, gr_s, qa_s, ka_s, va_s = _inproj(
        x_sample.reshape(Ts, D), msm[0], msm[1], norm1_g[0], w_in_bf, cos_s, sin_s, tms, 1, False)
    o_r_s, s_new = _retention(qr_s, kr_s, vr_s, state_ret[0], DB, L, L)
    o_a_s, nk, nv = _sample_attention(qa_s, ka_s, va_s, cache_win_k[0].reshape(DB, WB, ATT_W),
                                      cache_win_v[0].reshape(DB, WB, ATT_W), DB, L)

    T_all = Tp + Ts
    zero_cnt = jnp.zeros((1, N_EXPERTS), F32)
    x1_p, h2_p, ti_p, gt_p, rk_p, p8_p, meta_p, cnt_p = _merge(
        o_r_p, gr, o_a_p, x_prompt.reshape(Tp, D), mp[2], mp[3], mp[4], beta_ret[0], beta_att[0], w_out_bf,
        norm2_g[0], router_w[0], router_b[0], zero_cnt, tmp, S // tmp)
    x1_s, h2_s, ti_s, gt_s, rk_s, p8_s, meta_s, cnt = _merge(
        o_r_s, gr_s, o_a_s[None], x_sample.reshape(Ts, D), msm[2], msm[3], msm[4], beta_ret[0], beta_att[0],
        w_out_bf, norm2_g[0], router_w[0], router_b[0], cnt_p, tms, 1)

    top_i = jnp.concatenate([ti_p, ti_s], axis=0)
    rank = jnp.concatenate([rk_p, rk_s], axis=0)
    tme = min(MOE_TILE, T_all)
    dest, blk_exp, blk_new, n_used, fill_start, fill_n, pad_start, nb = _route(top_i, rank, cnt, tme)
    fill_max = 1 << (tme - 1).bit_length() >> 1
    xs = _dispatch(h2_p, h2_s, dest, fill_start, fill_n, nb * tme, tmt, fill_max)
    yb = _moe_experts(xs, blk_exp, blk_new, n_used, w_up[0], b_up[0], w_down[0], b_down[0], tme)

    def runs(meta):
        n = meta[:, 0, :].astype(jnp.int32)
        start = pad_start[None, :] + meta[:, 1, :].astype(jnp.int32)
        tot = jnp.sum((n + SEG - 1) // SEG, axis=1)
        return start.reshape(-1), n.reshape(-1), tot

    st_p, n_p, tot_p = runs(meta_p)
    st_s, n_s, tot_s = runs(meta_s)
    tcp, tcs = min(TOKEN_TILE, tmp), min(TOKEN_TILE, tms)
    assert tcs == Ts, "per-row modulation of the sample group needs one combine tile"
    y_p = _combine(st_p, n_p, tot_p, yb, p8_p, gt_p, x1_p, mp[5], fp[0], fp[1], final_norm_g, tcp, S // tcp)
    y_s = _combine(st_s, n_s, tot_s, yb, p8_s, gt_s, x1_s, msm[5], fs[0], fs[1], final_norm_g, tcs, 1)

    shp = (DB, WB, H_ATT, HD_ATT)
    return (y_p.reshape(B, S, D), y_s.reshape(DB, L, D), s_fin[None], s_new[None],
            wk_p[None], wv_p[None], nk.reshape(shp)[None], nv.reshape(shp)[None])
```

```python
import functools
import math

import numpy as np
import jax
import jax.numpy as jnp
from jax import lax
from jax.experimental import pallas as pl
from jax.experimental.pallas import tpu as pltpu

F32 = jnp.float32
BF16 = jnp.bfloat16
HIGHEST = lax.Precision.HIGHEST

D_MODEL = 1024
PAST_LEN = 16384
H_RET, DK_RET, DV_RET = 4, 64, 128
RET_CHUNK = 256
H_ATT, HD_ATT = 8, 64
DILATED_PAIRS = ((128, 1), (512, 4), (2048, 16))
ATT_BLOCK = 128
MAX_WINDOW = 2048
ROPE_THETA = 10000.0
RET_QK = H_RET * DK_RET
RET_V = H_RET * DV_RET
ATT_W = H_ATT * HD_ATT
IN_WIDTH = 2 * RET_QK + 2 * RET_V + 3 * ATT_W
N_EXPERTS = 32
TOP_K = 4
D_FF = 1024
SWIGLU_LIMIT = 7.0
SWIGLU_ALPHA = 1.702
N_MOD = 6
EPS = 1e-6

LANES = 128
VMEM_LIMIT = 56 * 1024 * 1024
ROW_TILE = 512
MOE_TILE = 512
TOKEN_TILE = 256
N_SLABS = ATT_W // LANES
ROW_CHUNKS = D_MODEL // LANES
SEG = 8
ATT_SPAN = 2048

RET_LOG_DECAY = tuple(math.log(1.0 - 2.0 ** (-5.0 - h)) for h in range(H_RET))


def _params(sem, vmem=VMEM_LIMIT):
    return pltpu.CompilerParams(dimension_semantics=sem, vmem_limit_bytes=vmem)


def _sigmoid(z):
    return 1.0 / (1.0 + jnp.exp(-z))


def _adaln_body(c_ref, w_ref, b_ref, o_ref):
    c = c_ref[...]
    s = c * _sigmoid(c)
    o_ref[...] = jnp.dot(s, w_ref[...], precision=HIGHEST, preferred_element_type=F32) + b_ref[...]


def _adaln(c, w, b):
    R, D = c.shape
    N = w.shape[1]
    tn = 1024
    return pl.pallas_call(
        _adaln_body,
        grid=(N // tn,),
        in_specs=[pl.BlockSpec((R, D), lambda j: (0, 0)),
                  pl.BlockSpec((D, tn), lambda j: (0, j)),
                  pl.BlockSpec((1, tn), lambda j: (0, j))],
        out_specs=pl.BlockSpec((R, tn), lambda j: (0, j)),
        out_shape=jax.ShapeDtypeStruct((R, N), F32),
        compiler_params=_params(("arbitrary",)),
        name="adaln",
    )(c, w, b.reshape(1, N))


def _rope_chunk(xc, cos, sin_signed, first_half):
    partner = jnp.where(first_half, pltpu.roll(xc, 96, 1), pltpu.roll(xc, 32, 1))
    return xc * cos + partner * sin_signed


def _inproj_body(x_ref, sh_ref, sc_ref, g_ref, w_ref, cos_ref, sin_ref, *refs, dilated):
    x = x_ref[...]
    tm = x.shape[0]
    ms = jnp.mean(x * x, axis=-1, keepdims=True)
    h = x * lax.rsqrt(ms + EPS) * g_ref[...]
    h = h * (1.0 + sc_ref[...]) + sh_ref[...]
    p = jnp.dot(h.astype(BF16), w_ref[...], preferred_element_type=F32)
    cos = cos_ref[...]
    sin = sin_ref[...]
    first_half = (lax.broadcasted_iota(jnp.int32, (1, LANES), 1) % HD_ATT) < (HD_ATT // 2)

    def chunk(off, c, rope):
        xc = p[:, off + c * LANES: off + (c + 1) * LANES]
        return _rope_chunk(xc, cos, sin, first_half) if rope else xc

    qr_ref, kr_ref, vr_ref, gr_ref = refs[:4]
    o = 0
    for c in range(RET_QK // LANES):
        qr_ref[:, c * LANES:(c + 1) * LANES] = chunk(o, c, True)
    o += RET_QK
    for c in range(RET_QK // LANES):
        kr_ref[:, c * LANES:(c + 1) * LANES] = chunk(o, c, True) * (DK_RET ** -0.5)
    o += RET_QK
    vr_ref[...] = p[:, o:o + RET_V]; o += RET_V
    gr_ref[...] = p[:, o:o + RET_V]; o += RET_V
    if not dilated:
        qa_ref, ka_ref, va_ref = refs[4:7]
        for ref, rope in ((qa_ref, True), (ka_ref, True), (va_ref, False)):
            for c in range(N_SLABS):
                ref[:, c * LANES:(c + 1) * LANES] = chunk(o, c, rope)
            o += ATT_W
        return
    ka_ref, va_ref = refs[4:6]
    slab_refs = refs[6:15]
    scr, scr4 = refs[15:17]
    n4 = tm // 4
    n16 = tm // 16
    for t, (nat_ref, rope, scale) in enumerate(((None, True, HD_ATT ** -0.5), (ka_ref, True, 1.0),
                                                (va_ref, False, 1.0))):
        s1_ref, s4_ref, s16_ref = slab_refs[t], slab_refs[3 + t], slab_refs[6 + t]
        for c in range(N_SLABS):
            xc = chunk(o, c, rope)
            if nat_ref is not None:
                nat_ref[:, c * LANES:(c + 1) * LANES] = xc
            if scale != 1.0:
                xc = xc * scale
            scr[...] = xc
            s1_ref[c] = xc.astype(BF16)
            for r in range(4):
                part = scr[pl.ds(r, n4, stride=4), :]
                s4_ref[c, r] = part.astype(BF16)
                scr4[r] = part
            for r in range(4):
                for r2 in range(4):
                    s16_ref[c, 4 * r2 + r] = scr4[r, pl.ds(r2, n16, stride=4), :].astype(BF16)
        o += ATT_W


def _mod_spec(mod, tm, tiles_per_group):
    R = mod.shape[1]
    return pl.BlockSpec((None, R, mod.shape[2]), lambda i: (i // tiles_per_group, 0, 0))


def _inproj(x2d, sh, sc, g, w_bf, cos_t, sin_t, tm, tiles_per_group, dilated, B=1):
    T, D = x2d.shape
    pos_tiles = cos_t.shape[0] // tm
    row = lambda w: pl.BlockSpec((tm, w), lambda i: (i, 0))
    tab = pl.BlockSpec((tm, LANES), lambda i: (i % pos_tiles, 0))
    f32 = lambda w: jax.ShapeDtypeStruct((T, w), F32)
    if dilated:
        S = T // B
        tpb = S // tm
        widths = (RET_QK, RET_QK, RET_V, RET_V, ATT_W, ATT_W)
        out_specs = [row(w) for w in widths]
        out_shape = [f32(w) for w in widths]
        for dil in (1, 4, 16):
            for _ in range(3):
                if dil == 1:
                    out_specs.append(pl.BlockSpec((None, N_SLABS, tm, LANES),
                                                  lambda i: (i // tpb, 0, i % tpb, 0)))
                    out_shape.append(jax.ShapeDtypeStruct((B, N_SLABS, S, LANES), BF16))
                else:
                    out_specs.append(pl.BlockSpec((None, N_SLABS, dil, tm // dil, LANES),
                                                  lambda i: (i // tpb, 0, 0, i % tpb, 0)))
                    out_shape.append(jax.ShapeDtypeStruct((B, N_SLABS, dil, S // dil, LANES), BF16))
        scratch = [pltpu.VMEM((tm, LANES), F32), pltpu.VMEM((4, tm // 4, LANES), F32)]
    else:
        widths = (RET_QK, RET_QK, RET_V, RET_V, ATT_W, ATT_W, ATT_W)
        out_specs = [row(w) for w in widths]
        out_shape = [f32(w) for w in widths]
        scratch = []
    return pl.pallas_call(
        functools.partial(_inproj_body, dilated=dilated),
        grid=(T // tm,),
        in_specs=[row(D), _mod_spec(sh, tm, tiles_per_group), _mod_spec(sc, tm, tiles_per_group),
                  pl.BlockSpec((1, D), lambda i: (0, 0)),
                  pl.BlockSpec((D, IN_WIDTH), lambda i: (0, 0)), tab, tab],
        out_specs=out_specs, out_shape=out_shape, scratch_shapes=scratch,
        compiler_params=_params(("arbitrary",)),
        name="inproj",
    )(x2d, sh, sc, g.reshape(1, D), w_bf, cos_t, sin_t)


def _rope_tables(pos):
    half = HD_ATT // 2
    inv = ROPE_THETA ** (-jnp.arange(half, dtype=F32) / half)
    ang = pos.astype(F32)[:, None] * inv[None, :]
    cos = jnp.cos(ang)
    sin = jnp.sin(ang)
    cos_t = jnp.tile(cos, (1, LANES // half))
    sin_t = jnp.tile(jnp.concatenate([-sin, sin], axis=1), (1, LANES // HD_ATT))
    return cos_t, sin_t


def _retention_body(q_ref, k_ref, v_ref, s0_ref, o_ref, sfin_ref, s_scr, *, L):
    c = pl.program_id(1)

    @pl.when(c == 0)
    def _():
        s_scr[...] = s0_ref[...]

    ii = lax.broadcasted_iota(jnp.int32, (L, L), 0)
    jj = lax.broadcasted_iota(jnp.int32, (L, L), 1)
    diff = (ii - jj).astype(F32)
    causal = diff >= 0
    row = lax.broadcasted_iota(jnp.int32, (L, 1), 0).astype(F32)
    for h in range(H_RET):
        lg = RET_LOG_DECAY[h]
        decay = jnp.where(causal, jnp.exp(jnp.where(causal, diff, 0.0) * lg), 0.0)
        q = q_ref[:, h * DK_RET:(h + 1) * DK_RET]
        k = k_ref[:, h * DK_RET:(h + 1) * DK_RET]
        v = v_ref[:, h * DV_RET:(h + 1) * DV_RET]
        qb, vb = q.astype(BF16), v.astype(BF16)
        scores = lax.dot_general(qb, k.astype(BF16), (((1,), (1,)), ((), ())),
                                 preferred_element_type=F32) * decay
        S = s_scr[h]
        o = jnp.dot(scores.astype(BF16), vb, preferred_element_type=F32)
        o = o + jnp.dot(qb, S.astype(BF16), preferred_element_type=F32) * jnp.exp((row + 1.0) * lg)
        k_dec = k * jnp.exp((L - 1.0 - row) * lg)
        s_scr[h] = math.exp(L * lg) * S + lax.dot_general(
            k_dec.astype(BF16), vb, (((0,), (0,)), ((), ())), preferred_element_type=F32)
        o_ref[:, h * DV_RET:(h + 1) * DV_RET] = o

    @pl.when(c == pl.num_programs(1) - 1)
    def _():
        sfin_ref[...] = s_scr[...]


def _retention(q2d, k2d, v2d, s0, B, L_seq, L):
    nc = L_seq // L
    row = lambda w: pl.BlockSpec((L, w), lambda b, c: (b * nc + c, 0))
    st = pl.BlockSpec((None, H_RET, DK_RET, DV_RET), lambda b, c: (b, 0, 0, 0))
    return pl.pallas_call(
        functools.partial(_retention_body, L=L),
        grid=(B, nc),
        in_specs=[row(RET_QK), row(RET_QK), row(RET_V), st],
        out_specs=[row(RET_V), st],
        out_shape=[jax.ShapeDtypeStruct((B * L_seq, RET_V), F32),
                   jax.ShapeDtypeStruct((B, H_RET, DK_RET, DV_RET), F32)],
        scratch_shapes=[pltpu.VMEM((H_RET, DK_RET, DV_RET), F32)],
        compiler_params=_params(("arbitrary", "arbitrary")),
        name="retention",
    )(q2d, k2d, v2d, s0)


def _band_attention(q, kcat, vcat, valid, low):
    nt = (((1,), (1,)), ((), ()))
    zero = jnp.zeros_like(q)
    outs = []
    for hh in range(2):
        qh = jnp.where(low if hh == 0 else ~low, q, zero)
        s = lax.dot_general(qh, kcat, nt, preferred_element_type=F32)
        s = jnp.where(valid, s, -jnp.inf)
        m = jnp.max(s, axis=-1, keepdims=True)
        p = jnp.exp(s - m)
        l = jnp.sum(p, axis=-1, keepdims=True)
        outs.append((jnp.dot(p.astype(BF16), vcat, preferred_element_type=F32), m, l))
    (a0, m0, l0), (a1, m1, l1) = outs
    shape = a0.shape
    return (jnp.where(low, a0, a1), jnp.where(low, jnp.broadcast_to(m0, shape), jnp.broadcast_to(m1, shape)),
            jnp.where(low, jnp.broadcast_to(l0, shape), jnp.broadcast_to(l1, shape)))


def _dilated_body(q1, k1, k1p, v1, v1p, q4, k4, k4p, v4, v4p, q16, k16, k16p, v16, v16p,
                  o_ref, kf1, vf1, kf4, vf4, acc_s, m_s, l_s):
    j = pl.program_id(2)
    nq = ATT_BLOCK
    n1 = ATT_SPAN // nq
    GROUP = 4
    kf1[0:nq] = k1p[...]
    kf1[nq:] = k1[...]
    vf1[0:nq] = v1p[...]
    vf1[nq:] = v1[...]
    for r in range(4):
        kf4[r, 0:nq] = k4p[r]
        kf4[r, nq:] = k4[r]
        vf4[r, 0:nq] = v4p[r]
        vf4[r, nq:] = v4[r]
    qi = lax.broadcasted_iota(jnp.int32, (nq, 2 * nq), 0) + nq
    ki = lax.broadcasted_iota(jnp.int32, (nq, 2 * nq), 1)
    dist = qi - ki
    band = (dist >= 0) & (dist <= nq)
    in_cur = ki >= nq
    low = lax.broadcasted_iota(jnp.int32, (1, LANES), 1) < HD_ATT

    def mask(has_prev):
        return band & (has_prev | in_cur)

    def merge(rows, a, m, l):
        m_old, l_old = m_s[rows, :], l_s[rows, :]
        m_new = jnp.maximum(m_old, m)
        w_old, w_new = jnp.exp(m_old - m_new), jnp.exp(m - m_new)
        return acc_s[rows, :] * w_old + a * w_new, m_new, l_old * w_old + l * w_new

    def dil1(g, carry):
        res = []
        for u in range(GROUP):
            qb = g * GROUP + u
            r0 = pl.multiple_of(qb * nq, nq)
            res.append((pl.ds(r0, nq), _band_attention(
                q1[pl.ds(r0, nq), :], kf1[pl.ds(r0, 2 * nq), :], vf1[pl.ds(r0, 2 * nq), :],
                mask((j > 0) | (qb > 0)), low)))
        for rows, (a, m, l) in res:
            acc_s[rows, :] = a
            m_s[rows, :] = m
            l_s[rows, :] = l
        return carry

    lax.fori_loop(0, n1 // GROUP, dil1, 0)

    def dil4(qb, carry):
        r0 = pl.multiple_of(qb * nq, nq)
        res = []
        for r in range(4):
            res.append((pl.ds(qb * (4 * nq) + r, nq, stride=4), _band_attention(
                q4[r, pl.ds(r0, nq), :], kf4[r, pl.ds(r0, 2 * nq), :], vf4[r, pl.ds(r0, 2 * nq), :],
                mask((j > 0) | (qb > 0)), low)))
        for rows, (a, m, l) in res:
            a, m, l = merge(rows, a, m, l)
            acc_s[rows, :] = a
            m_s[rows, :] = m
            l_s[rows, :] = l
        return carry

    lax.fori_loop(0, ATT_SPAN // 4 // nq, dil4, 0)

    def dil16(g, carry):
        res = []
        for u in range(GROUP):
            r = g * GROUP + u
            kcat = jnp.concatenate([k16p[r], k16[r]], axis=0)
            vcat = jnp.concatenate([v16p[r], v16[r]], axis=0)
            res.append((pl.ds(r, nq, stride=16), _band_attention(q16[r], kcat, vcat, mask(j > 0), low)))
        for rows, (a, m, l) in res:
            a, m, l = merge(rows, a, m, l)
            o_ref[rows, :] = a / l
        return carry

    lax.fori_loop(0, 16 // GROUP, dil16, 0)


def _dilated_prompt(slabs, B, S):
    q1, k1, v1, q4, k4, v4, q16, k16, v16 = slabs
    assert S % ATT_SPAN == 0 and DILATED_PAIRS == ((128, 1), (512, 4), (2048, 16))
    nq = ATT_BLOCK
    prev = lambda j, per: jnp.maximum(j * per - 1, 0)
    s1 = pl.BlockSpec((None, None, ATT_SPAN, LANES), lambda b, c, j: (b, c, j, 0))
    s1p = pl.BlockSpec((None, None, nq, LANES), lambda b, c, j: (b, c, prev(j, ATT_SPAN // nq), 0))
    s4 = pl.BlockSpec((None, None, 4, ATT_SPAN // 4, LANES), lambda b, c, j: (b, c, 0, j, 0))
    s4p = pl.BlockSpec((None, None, 4, nq, LANES), lambda b, c, j: (b, c, 0, prev(j, ATT_SPAN // 4 // nq), 0))
    s16 = pl.BlockSpec((None, None, 16, nq, LANES), lambda b, c, j: (b, c, 0, j, 0))
    s16p = pl.BlockSpec((None, None, 16, nq, LANES), lambda b, c, j: (b, c, 0, prev(j, 1), 0))
    return pl.pallas_call(
        _dilated_body,
        grid=(B, N_SLABS, S // ATT_SPAN),
        in_specs=[s1, s1, s1p, s1, s1p, s4, s4, s4p, s4, s4p, s16, s16, s16p, s16, s16p],
        out_specs=pl.BlockSpec((None, None, ATT_SPAN, LANES), lambda b, c, j: (b, c, j, 0)),
        out_shape=jax.ShapeDtypeStruct((B, N_SLABS, S, LANES), F32),
        scratch_shapes=[pltpu.VMEM((ATT_SPAN + nq, LANES), BF16), pltpu.VMEM((ATT_SPAN + nq, LANES), BF16),
                        pltpu.VMEM((4, ATT_SPAN // 4 + nq, LANES), BF16),
                        pltpu.VMEM((4, ATT_SPAN // 4 + nq, LANES), BF16),
                        pltpu.VMEM((ATT_SPAN, LANES), F32), pltpu.VMEM((ATT_SPAN, LANES), F32),
                        pltpu.VMEM((ATT_SPAN, LANES), F32)],
        compiler_params=_params(("arbitrary", "arbitrary", "arbitrary")),
        name="dilated_attention",
    )(q1, k1, k1, v1, v1, q4, k4, k4, v4, v4, q16, k16, k16, v16, v16)


def _sample_multiplicity(L, WB):
    pad = LANES
    cnt = np.zeros((L, WB + pad), np.float32)
    for l in range(L):
        for window, dil in DILATED_PAIRS:
            for j in range(window // dil + 1):
                idx = WB + l - dil * j
                if idx >= 0:
                    cnt[l, idx] += 1.0
    cnt = np.tile(cnt, (H_ATT, 1))
    return cnt[:, :WB], cnt[:, WB:]


def _sample_att_body(q_ref, kn_ref, vn_ref, bk_ref, bv_ref, cb_ref, cn_ref,
                     o_ref, nk_ref, nv_ref, *, L, WB):
    q = q_ref[...]
    lane_head = lax.broadcasted_iota(jnp.int32, (1, ATT_W), 1) // HD_ATT
    qexp = jnp.concatenate([jnp.where(lane_head == h, q, 0.0) for h in range(H_ATT)], axis=0).astype(BF16)
    zpad = jnp.zeros((LANES - L, ATT_W), F32)
    kn = jnp.concatenate([kn_ref[...], zpad], axis=0).astype(BF16)
    vn = jnp.concatenate([vn_ref[...], zpad], axis=0).astype(BF16)
    bk = bk_ref[...]
    bv = bv_ref[...]
    nt = (((1,), (1,)), ((), ()))
    scale = HD_ATT ** -0.5
    sb = lax.dot_general(qexp, bk.astype(BF16), nt, preferred_element_type=F32) * scale
    sn = lax.dot_general(qexp, kn, nt, preferred_element_type=F32) * scale
    cb = cb_ref[...]
    cn = cn_ref[...]
    sb = jnp.where(cb > 0, sb, -jnp.inf)
    sn = jnp.where(cn > 0, sn, -jnp.inf)
    m = jnp.maximum(jnp.max(sb, axis=-1, keepdims=True), jnp.max(sn, axis=-1, keepdims=True))
    pb = cb * jnp.exp(sb - m)
    pn = cn * jnp.exp(sn - m)
    l = jnp.sum(pb, axis=-1, keepdims=True) + jnp.sum(pn, axis=-1, keepdims=True)
    o_all = (jnp.dot(pb.astype(BF16), bv.astype(BF16), preferred_element_type=F32)
             + jnp.dot(pn.astype(BF16), vn, preferred_element_type=F32)) / l
    o = jnp.zeros((L, ATT_W), F32)
    for h in range(H_ATT):
        o = o + jnp.where(lane_head == h, o_all[h * L:(h + 1) * L], 0.0)
    for c in range(N_SLABS):
        o_ref[c] = o[:, c * LANES:(c + 1) * LANES]
    nk_ref[0:WB - L, :] = bk[L:WB]
    nk_ref[WB - L:WB, :] = kn_ref[...]
    nv_ref[0:WB - L, :] = bv[L:WB]
    nv_ref[WB - L:WB, :] = vn_ref[...]


def _sample_attention(q2d, k2d, v2d, buf_k, buf_v, DB, L):
    WB = buf_k.shape[1]
    cb, cn = _sample_multiplicity(L, WB)
    row = pl.BlockSpec((L, ATT_W), lambda b: (b, 0))
    buf = pl.BlockSpec((None, WB, ATT_W), lambda b: (b, 0, 0))
    const = lambda a: pl.BlockSpec(a.shape, lambda b: (0, 0))
    return pl.pallas_call(
        functools.partial(_sample_att_body, L=L, WB=WB),
        grid=(DB,),
        in_specs=[row, row, row, buf, buf, const(cb), const(cn)],
        out_specs=[pl.BlockSpec((N_SLABS, L, LANES), lambda b: (0, b, 0)), buf, buf],
        out_shape=[jax.ShapeDtypeStruct((N_SLABS, DB * L, LANES), F32),
                   jax.ShapeDtypeStruct((DB, WB, ATT_W), F32),
                   jax.ShapeDtypeStruct((DB, WB, ATT_W), F32)],
        compiler_params=_params(("arbitrary",)),
        name="sample_attention",
    )(q2d, k2d, v2d, buf_k, buf_v, jnp.asarray(cb), jnp.asarray(cn))


def _group_mean(sq, width):
    if width == LANES:
        return jnp.mean(sq, axis=-1, keepdims=True)
    lane = lax.broadcasted_iota(jnp.int32, (1, LANES), 1)
    low = lane < width
    s_lo = jnp.sum(jnp.where(low, sq, 0.0), axis=-1, keepdims=True)
    s_hi = jnp.sum(jnp.where(low, 0.0, sq), axis=-1, keepdims=True)
    return jnp.where(low, s_lo, s_hi) * (1.0 / width)


def _merge_body(or_ref, gr_ref, oa0, oa1, oa2, oa3, x_ref, g1_ref, sh_ref, sc_ref, br_ref, ba_ref, wo_ref,
                n2_ref, rw_ref, rb_ref, cin_ref,
                x1_ref, h2_ref, gt_ref, p8_ref, p8t_ref, meta_ref, cnt_ref, cnt_scr):
    i = pl.program_id(0)

    @pl.when(i == 0)
    def _():
        cnt_scr[...] = cin_ref[...]

    parts = []
    for c in range(RET_V // LANES):
        cs = slice(c * LANES, (c + 1) * LANES)
        o = or_ref[:, cs]
        o = o * lax.rsqrt(_group_mean(o * o, DV_RET) + EPS) * br_ref[:, cs]
        g = gr_ref[:, cs]
        parts.append((o * (g * _sigmoid(g))).astype(BF16))
    for c, oa in enumerate((oa0, oa1, oa2, oa3)):
        cs = slice(c * LANES, (c + 1) * LANES)
        o = oa[...]
        o = o * lax.rsqrt(_group_mean(o * o, HD_ATT) + EPS) * ba_ref[:, cs]
        parts.append(o.astype(BF16))
    mix = jnp.concatenate(parts, axis=1)
    y = jnp.dot(mix, wo_ref[...], preferred_element_type=F32)
    x1 = x_ref[...] + g1_ref[...] * y
    x1_ref[...] = x1
    ms = jnp.mean(x1 * x1, axis=-1, keepdims=True)
    h2 = x1 * lax.rsqrt(ms + EPS) * n2_ref[...]
    h2 = h2 * (1.0 + sc_ref[...]) + sh_ref[...]
    rw = rw_ref[...]
    h_hi = h2.astype(BF16)
    h2_ref[...] = h_hi
    h_lo = (h2 - h_hi.astype(F32)).astype(BF16)
    w_hi = rw.astype(BF16)
    w_lo = (rw - w_hi.astype(F32)).astype(BF16)
    logits = (jnp.dot(h_hi, w_hi, preferred_element_type=F32) + jnp.dot(h_lo, w_hi, preferred_element_type=F32)
              + jnp.dot(h_hi, w_lo, preferred_element_type=F32)) + rb_ref[...]
    tm = logits.shape[0]
    lane = lax.broadcasted_iota(jnp.int32, logits.shape, 1).astype(F32)
    vals, idxs = [], []
    for _ in range(TOP_K):
        m = jnp.max(logits, axis=-1, keepdims=True)
        idx = jnp.min(jnp.where(logits == m, lane, float(N_EXPERTS)), axis=-1, keepdims=True)
        vals.append(m)
        idxs.append(idx)
        logits = jnp.where(lane == idx, -jnp.inf, logits)
    e = [jnp.exp(v - vals[0]) for v in vals]
    tot = e[0] + e[1] + e[2] + e[3]
    for k in range(TOP_K):
        gt_ref[:, k:k + 1] = e[k] / tot
    member = jnp.zeros(logits.shape, F32)
    for k in range(TOP_K):
        member = member + jnp.where(lane == idxs[k], 1.0, 0.0)
    sub = min(TOKEN_TILE, tm)
    ti = lax.broadcasted_iota(jnp.int32, (tm, tm), 0)
    tj = lax.broadcasted_iota(jnp.int32, (tm, tm), 1)
    before = jnp.where((tj < ti) & (tj // sub == ti // sub), 1.0, 0.0).astype(BF16)
    rin = jnp.dot(before, member.astype(BF16), preferred_element_type=F32)
    earlier = jnp.where(lax.broadcasted_iota(jnp.int32, (N_EXPERTS, N_EXPERTS), 0)
                        < lax.broadcasted_iota(jnp.int32, (N_EXPERTS, N_EXPERTS), 1), 1.0, 0.0)
    row_sub = lax.broadcasted_iota(jnp.int32, (tm, 1), 0) // sub
    carry = cnt_scr[...]
    off8 = jnp.zeros(logits.shape, F32)
    for s in range(tm // sub):
        in_sub = row_sub == s
        n = jnp.sum(jnp.where(in_sub, member, 0.0), axis=0, keepdims=True)
        n8 = jnp.floor((n + 7.0) * 0.125) * 8.0
        o8 = jnp.dot(jnp.broadcast_to(n8, (8, N_EXPERTS)), earlier, precision=HIGHEST,
                     preferred_element_type=F32)
        o8 = jnp.max(o8, axis=0, keepdims=True)
        off8 = off8 + jnp.where(in_sub, o8, 0.0)
        meta_ref[s, 0:1, :] = n
        meta_ref[s, 1:2, :] = carry
        carry = carry + n
    lane128 = lax.broadcasted_iota(jnp.int32, (tm, LANES), 1)
    slots_k = jnp.zeros((tm, LANES), F32)
    for k in range(TOP_K):
        pos = jnp.sum(jnp.where(lane == idxs[k], rin + off8, 0.0), axis=-1, keepdims=True)
        p8_ref[:, k:k + 1] = pos.astype(jnp.int32)
        slots_k = slots_k + jnp.where(lane128 == k, pos, 0.0)
    p8t_ref[...] = slots_k.T[0:8, :].astype(jnp.int32)
    cnt_scr[...] = carry
    cnt_ref[...] = carry


def _merge(o_r, g_r, o_a, x2d, g1, sh2, sc2, beta_ret, beta_att, wo_bf, n2, rw, rb, cnt_in, tm,
           tiles_per_group):
    T, D = x2d.shape
    tpg = o_a.shape[2] // tm
    sub = min(TOKEN_TILE, tm)
    row = lambda w: pl.BlockSpec((tm, w), lambda i: (i, 0))
    const = lambda r, c: pl.BlockSpec((r, c), lambda i: (0, 0))
    ms = lambda m: _mod_spec(m, tm, tiles_per_group)
    slab = lambda c: pl.BlockSpec((None, None, tm, LANES), lambda i: (i // tpg, c, i % tpg, 0))
    ins = [o_r, g_r, o_a, o_a, o_a, o_a, x2d, g1, sh2, sc2, beta_ret.reshape(1, RET_V),
           beta_att.reshape(1, ATT_W), wo_bf, n2.reshape(1, D), rw, rb.reshape(1, N_EXPERTS), cnt_in]
    in_specs = [row(RET_V), row(RET_V), slab(0), slab(1), slab(2), slab(3), row(D), ms(g1), ms(sh2), ms(sc2),
                const(1, RET_V), const(1, ATT_W), const(D, D), const(1, D),
                const(D, N_EXPERTS), const(1, N_EXPERTS), const(1, N_EXPERTS)]
    return pl.pallas_call(
        _merge_body,
        grid=(T // tm,),
        in_specs=in_specs,
        out_specs=[row(D), row(D), row(TOP_K), row(TOP_K), pl.BlockSpec((8, tm), lambda i: (0, i)),
                   pl.BlockSpec((tm // sub, 2, N_EXPERTS), lambda i: (i, 0, 0)), const(1, N_EXPERTS)],
        out_shape=[jax.ShapeDtypeStruct((T, D), F32), jax.ShapeDtypeStruct((T, D), BF16),
                   jax.ShapeDtypeStruct((T, TOP_K), F32), jax.ShapeDtypeStruct((T, TOP_K), jnp.int32),
                   jax.ShapeDtypeStruct((8, T), jnp.int32),
                   jax.ShapeDtypeStruct((T // sub, 2, N_EXPERTS), F32),
                   jax.ShapeDtypeStruct((1, N_EXPERTS), F32)],
        scratch_shapes=[pltpu.VMEM((1, N_EXPERTS), F32)],
        compiler_params=_params(("arbitrary",)),
        name="merge_router",
    )(*ins)


def _slots(tile):
    return -(-(tile * TOP_K + N_EXPERTS * SEG) // LANES) * LANES


def _route(counts, meta, n_assign, tm, max_chunks):
    nb = -(-(n_assign + N_EXPERTS * (tm - 1 + SEG - 1)) // tm)
    experts = jnp.arange(N_EXPERTS, dtype=jnp.int32)
    counts = counts.reshape(N_EXPERTS).astype(jnp.int32)
    padded = (counts + (SEG - 1) + tm - 1) // tm * tm
    pad_end = jnp.cumsum(padded)
    pad_start = pad_end - padded
    blk0 = jnp.arange(nb + 1, dtype=jnp.int32) * tm
    blk_exp = jnp.minimum(jnp.sum((pad_end[None, :] <= blk0[:, None]).astype(jnp.int32), axis=1),
                          N_EXPERTS - 1).astype(jnp.int32)
    n_used = (pad_end[-1:] // tm).astype(jnp.int32)
    blk_new = jnp.concatenate([jnp.ones((1,), jnp.int32),
                               (blk_exp[1:] != blk_exp[:-1]).astype(jnp.int32)])
    fill_start = jnp.concatenate([pad_start + counts, pad_end[-1:]]).astype(jnp.int32)
    fill_n = jnp.concatenate([padded - counts, nb * tm - pad_end[-1:]]).astype(jnp.int32)
    n = meta[:, 0, :].astype(jnp.int32)
    start = pad_start[None, :] + meta[:, 1, :].astype(jnp.int32)
    nseg = (n + SEG - 1) // SEG
    cum = jnp.cumsum(nseg, axis=1)
    m = jnp.arange(max_chunks, dtype=jnp.int32)
    e_of = jnp.sum((cum[:, None, :] <= m[None, :, None]).astype(jnp.int32), axis=-1)
    onehot = e_of[:, :, None] == experts[None, None, :]
    first = jnp.sum(jnp.where(onehot, (cum - nseg)[:, None, :], 0), axis=-1)
    st = jnp.sum(jnp.where(onehot, start[:, None, :], 0), axis=-1)
    chunk_row = (st + SEG * (m[None, :] - first)).astype(jnp.int32)
    return blk_exp, blk_new, n_used, fill_start, fill_n, chunk_row, cum[:, -1].astype(jnp.int32), nb


def _wait_copies(count, sem, src, dst, rows_per_copy, max_count):
    sz = 1 << (max_count.bit_length() - 1)
    while sz >= 1:
        @pl.when((count & sz) != 0)
        def _(sz=sz):
            pltpu.make_async_copy(src.at[pl.ds(0, sz * rows_per_copy)], dst.at[pl.ds(0, sz * rows_per_copy)],
                                  sem).wait()
        sz //= 2


def _dispatch_body(rows_ref, tot_ref, fs_ref, fn_ref, p8t_ref, hp_ref, hs_ref, xs_hbm,
                   sbuf, zbuf, sem, zsem, *, tm, slots, max_chunks, fill_max, prompt_tiles):
    i = pl.program_id(0)
    slot = i % 2
    seg_rows = SEG * ROW_CHUNKS

    @pl.when(i > 0)
    def _():
        _wait_copies(tot_ref[i - 1], sem.at[1 - slot], sbuf.at[1 - slot], xs_hbm, seg_rows, max_chunks)

    h = jnp.where(i < prompt_tiles, hp_ref[...], hs_ref[...])
    s_iota = lax.broadcasted_iota(jnp.int32, (slots, tm), 0)
    p = jnp.zeros((slots, tm), F32)
    for k in range(TOP_K):
        p = p + jnp.where(s_iota == p8t_ref[k:k + 1, :], 1.0, 0.0)
    srt = jnp.dot(p.astype(BF16), h, preferred_element_type=F32)
    for c in range(ROW_CHUNKS):
        sbuf[slot, pl.ds(c, slots, stride=ROW_CHUNKS), :] = srt[:, c * LANES:(c + 1) * LANES]

    def one(m, carry):
        row = pl.multiple_of(rows_ref[i * max_chunks + m] * ROW_CHUNKS, ROW_CHUNKS)
        src = pl.multiple_of(m * seg_rows, seg_rows)
        pltpu.make_async_copy(sbuf.at[slot, pl.ds(src, seg_rows)], xs_hbm.at[pl.ds(row, seg_rows)],
                              sem.at[slot]).start()
        return carry

    lax.fori_loop(0, tot_ref[i], one, 0)

    @pl.when(i == pl.num_programs(0) - 1)
    def _():
        _wait_copies(tot_ref[i], sem.at[slot], sbuf.at[slot], xs_hbm, seg_rows, max_chunks)
        zbuf[...] = jnp.zeros(zbuf.shape, F32)

        def fill(e, wait):
            start = fs_ref[e]
            n = fn_ref[e]
            sz = fill_max
            while sz >= 1:
                @pl.when((n & sz) != 0)
                def _(start=start, sz=sz):
                    cp = pltpu.make_async_copy(
                        zbuf.at[pl.ds(0, sz * ROW_CHUNKS)],
                        xs_hbm.at[pl.ds(pl.multiple_of(start * ROW_CHUNKS, ROW_CHUNKS), sz * ROW_CHUNKS)], zsem)
                    cp.wait() if wait else cp.start()
                start = start + (n & sz)
                sz //= 2

        def fill_all(wait):
            def body(e, carry):
                fill(e, wait)
                return carry
            lax.fori_loop(0, N_EXPERTS, body, 0)

            def tail(c, carry):
                row = pl.multiple_of((fs_ref[N_EXPERTS] + c * fill_max) * ROW_CHUNKS, ROW_CHUNKS)
                cp = pltpu.make_async_copy(zbuf, xs_hbm.at[pl.ds(row, fill_max * ROW_CHUNKS)], zsem)
                cp.wait() if wait else cp.start()
                return carry
            lax.fori_loop(0, fn_ref[N_EXPERTS] // fill_max, tail, 0)

        fill_all(False)
        fill_all(True)


def _dispatch(h2_p, h2_s, p8t, chunk_row, tot, fill_start, fill_n, R, tm, fill_max):
    Tp, D = h2_p.shape
    Ts = h2_s.shape[0]
    np_, ns_ = Tp // tm, Ts // tm
    slots = _slots(tm)
    max_chunks = slots // SEG
    grid_spec = pltpu.PrefetchScalarGridSpec(
        num_scalar_prefetch=4,
        grid=(np_ + ns_,),
        in_specs=[pl.BlockSpec((8, tm), lambda i, *_: (0, i)),
                  pl.BlockSpec((tm, D), lambda i, *_: (jnp.minimum(i, np_ - 1), 0)),
                  pl.BlockSpec((tm, D), lambda i, *_: (jnp.maximum(i - np_, 0), 0))],
        out_specs=pl.BlockSpec(memory_space=pl.ANY),
        scratch_shapes=[pltpu.VMEM((2, slots * ROW_CHUNKS, LANES), F32),
                        pltpu.VMEM((fill_max * ROW_CHUNKS, LANES), F32),
                        pltpu.SemaphoreType.DMA((2,)), pltpu.SemaphoreType.DMA(())],
    )
    return pl.pallas_call(
        functools.partial(_dispatch_body, tm=tm, slots=slots, max_chunks=max_chunks, fill_max=fill_max,
                          prompt_tiles=np_),
        grid_spec=grid_spec,
        out_shape=jax.ShapeDtypeStruct((R * ROW_CHUNKS, LANES), F32),
        compiler_params=_params(("arbitrary",)),
        name="moe_dispatch",
    )(chunk_row.reshape(-1), tot, fill_start, fill_n, p8t, h2_p, h2_s)


def _moe_body(nused_ref, bexp_ref, new_ref, x_ref, wu_ref, bu_ref, wd_ref, bd_ref, y_ref, wu_bf, wd_bf):
    i = pl.program_id(0)

    @pl.when(new_ref[i] != 0)
    def _():
        wu_bf[...] = wu_ref[...].astype(BF16)
        wd_bf[...] = wd_ref[...].astype(BF16)

    @pl.when(i < nused_ref[0])
    def _():
        tm = x_ref.shape[0] // ROW_CHUNKS
        x = jnp.concatenate([x_ref[pl.ds(c, tm, stride=ROW_CHUNKS), :].astype(BF16)
                             for c in range(ROW_CHUNKS)], axis=1)
        u = jnp.dot(x, wu_bf[...], preferred_element_type=F32) + bu_ref[...]
        x_glu = jnp.minimum(u[:, :D_FF], SWIGLU_LIMIT)
        x_lin = jnp.clip(u[:, D_FF:], -SWIGLU_LIMIT, SWIGLU_LIMIT)
        act = x_glu * _sigmoid(SWIGLU_ALPHA * x_glu) * (x_lin + 1.0)
        y = jnp.dot(act.astype(BF16), wd_bf[...], preferred_element_type=F32) + bd_ref[...]
        for c in range(ROW_CHUNKS):
            y_ref[pl.ds(c, tm, stride=ROW_CHUNKS), :] = y[:, c * LANES:(c + 1) * LANES]

    @pl.when(i >= nused_ref[0])
    def _():
        y_ref[...] = jnp.zeros(y_ref.shape, F32)


def _moe_experts(xs, blk_exp, blk_new, n_used, w_up, bu, w_down, bd, tm):
    R, D = xs.shape[0] // ROW_CHUNKS, D_MODEL
    blk = lambda i, nu, be, nw: (jnp.minimum(i, nu[0] - 1), 0)
    wspec = lambda r, c: pl.BlockSpec((None, r, c), lambda i, nu, be, nw: (be[i], 0, 0))
    grid_spec = pltpu.PrefetchScalarGridSpec(
        num_scalar_prefetch=3,
        grid=(R // tm + 1,),
        in_specs=[pl.BlockSpec((tm * ROW_CHUNKS, LANES), blk), wspec(D, 2 * D_FF), wspec(1, 2 * D_FF),
                  wspec(D_FF, D), wspec(1, D)],
        out_specs=pl.BlockSpec((tm * ROW_CHUNKS, LANES), lambda i, nu, be, nw: (i, 0)),
        scratch_shapes=[pltpu.VMEM((D, 2 * D_FF), BF16), pltpu.VMEM((D_FF, D), BF16)],
    )
    return pl.pallas_call(
        _moe_body,
        grid_spec=grid_spec,
        out_shape=jax.ShapeDtypeStruct(((R + tm) * ROW_CHUNKS, LANES), F32),
        compiler_params=_params(("arbitrary",)),
        name="moe_experts",
    )(n_used, blk_exp, blk_new, xs, w_up, bu.reshape(N_EXPERTS, 1, 2 * D_FF), w_down,
      bd.reshape(N_EXPERTS, 1, D))


def _combine_body(rows_ref, tot_ref, yb_hbm, p8_ref, gate_ref, x1_ref, g2_ref, shf_ref, scf_ref, gf_ref,
                  out_ref, sbuf, sem, *, tm, slots, max_chunks):
    i = pl.program_id(0)
    nsteps = pl.num_programs(0)
    seg_rows = SEG * ROW_CHUNKS

    @pl.when(i == 0)
    def _():
        sbuf[...] = jnp.zeros(sbuf.shape, F32)

    def start_runs(tile, slot):
        def one(m, carry):
            src = pl.multiple_of(rows_ref[tile * max_chunks + m] * ROW_CHUNKS, ROW_CHUNKS)
            dst = pl.multiple_of(m * seg_rows, seg_rows)
            pltpu.make_async_copy(yb_hbm.at[pl.ds(src, seg_rows)], sbuf.at[slot, pl.ds(dst, seg_rows)],
                                  sem.at[slot]).start()
            return carry
        lax.fori_loop(0, tot_ref[tile], one, 0)

    @pl.when(i == 0)
    def _():
        start_runs(0, 0)

    @pl.when(i + 1 < nsteps)
    def _():
        start_runs(i + 1, (i + 1) % 2)

    slot = i % 2
    _wait_copies(tot_ref[i], sem.at[slot], yb_hbm, sbuf.at[slot], seg_rows, max_chunks)
    pos = p8_ref[...]
    gates = gate_ref[...]
    s_iota = lax.broadcasted_iota(jnp.int32, (tm, slots), 1)
    g = jnp.zeros((tm, slots), F32)
    for k in range(TOP_K):
        g = g + jnp.where(s_iota == pos[:, k:k + 1], gates[:, k:k + 1], 0.0)
    gb = g.astype(BF16)
    ys = []
    for c in range(ROW_CHUNKS):
        chunk = sbuf[slot, pl.ds(c, slots, stride=ROW_CHUNKS), :]
        ys.append(jnp.dot(gb, chunk.astype(BF16), preferred_element_type=F32))
    y = jnp.concatenate(ys, axis=1)
    x2 = x1_ref[...] + g2_ref[...] * y
    ms = jnp.mean(x2 * x2, axis=-1, keepdims=True)
    h = x2 * lax.rsqrt(ms + EPS) * gf_ref[...]
    out_ref[...] = h * (1.0 + scf_ref[...]) + shf_ref[...]


def _combine(chunk_row, tot, yb, pos8, gates, x1, g2, shf, scf, gf, tm, tiles_per_group):
    T, D = x1.shape
    nsteps = T // tm
    slots = _slots(tm)
    max_chunks = slots // SEG
    row = lambda w: pl.BlockSpec((tm, w), lambda i, rows, tot: (i, 0))
    ms = lambda m: pl.BlockSpec((None, m.shape[1], m.shape[2]),
                                lambda i, rows, tot: (i // tiles_per_group, 0, 0))
    grid_spec = pltpu.PrefetchScalarGridSpec(
        num_scalar_prefetch=2,
        grid=(nsteps,),
        in_specs=[pl.BlockSpec(memory_space=pl.ANY), row(TOP_K), row(TOP_K), row(D), ms(g2), ms(shf), ms(scf),
                  pl.BlockSpec((1, D), lambda i, rows, tot: (0, 0))],
        out_specs=row(D),
        scratch_shapes=[pltpu.VMEM((2, slots * ROW_CHUNKS, LANES), F32), pltpu.SemaphoreType.DMA((2,))],
    )
    return pl.pallas_call(
        functools.partial(_combine_body, tm=tm, slots=slots, max_chunks=max_chunks),
        grid_spec=grid_spec,
        out_shape=jax.ShapeDtypeStruct((T, D), F32),
        compiler_params=_params(("arbitrary",)),
        name="combine_final",
    )(chunk_row.reshape(-1), tot, yb, pos8, gates, x1, g2, shf, scf, gf.reshape(1, D))


def _group_mods(mod, n, rows_per_group, per_row):
    outs = []
    for j in range(n):
        m = mod[:, j * D_MODEL:(j + 1) * D_MODEL]
        if per_row:
            outs.append(jnp.repeat(m, rows_per_group, axis=0)[None])
        else:
            outs.append(m[:, None, :])
    return outs


def kernel(x_prompt, x_sample, state_ret, cache_win_k, cache_win_v, c_prompt, c_sample,
           ada_w, ada_b, norm1_g, w_in, beta_ret, beta_att, w_out, norm2_g,
           router_w, router_b, w_up, b_up, w_down, b_down,
           final_ada_w, final_ada_b, final_norm_g):
    B, S, D = x_prompt.shape
    DB, L = x_sample.shape[:2]
    WB = cache_win_k.shape[2]
    Tp, Ts = B * S, DB * L
    assert ada_w.shape[0] == 1, "single-layer trunk"
    c_all = jnp.concatenate([c_prompt, c_sample], axis=0)
    c_all = jnp.pad(c_all, ((0, (-c_all.shape[0]) % 8), (0, 0)))
    mods = _adaln(c_all, ada_w[0], ada_b[0])
    fmods = _adaln(c_all, final_ada_w, final_ada_b)
    w_in_bf, w_out_bf = w_in[0].astype(BF16), w_out[0].astype(BF16)

    tmp = min(ROW_TILE, S)
    tms = Ts
    assert Tp % tmp == 0
    mp = _group_mods(mods[:B], N_MOD, S, False)
    msm = _group_mods(mods[B:B + DB], N_MOD, L, True)
    fp = _group_mods(fmods[:B], 2, S, False)
    fs = _group_mods(fmods[B:B + DB], 2, L, True)

    cos_p, sin_p = _rope_tables(jnp.arange(S))
    outs = _inproj(x_prompt.reshape(Tp, D), mp[0], mp[1], norm1_g[0], w_in_bf, cos_p, sin_p,
                   tmp, S // tmp, True, B)
    qr, kr, vr, gr, ka, va = outs[:6]
    o_r_p, s_fin = _retention(qr, kr, vr, jnp.zeros((B, H_RET, DK_RET, DV_RET), F32), B, S, RET_CHUNK)
    o_a_p = _dilated_prompt(outs[6:], B, S)
    keep = min(MAX_WINDOW, S)
    win = lambda t: t.reshape(B, S, ATT_W)[:, S - keep:].reshape(B, keep, H_ATT, HD_ATT)
    wk_p, wv_p = win(ka), win(va)

    cos_s, sin_s = _rope_tables(PAST_LEN + jnp.arange(L))
    cos_s, sin_s = jnp.tile(cos_s, (DB, 1)), jnp.tile(sin_s, (DB, 1))
    qr_s, kr_s, vr_s, gr_s, qa_s, ka_s, va_s = _inproj(
        x_sample.reshape(Ts, D), msm[0], msm[1], norm1_g[0], w_in_bf, cos_s, sin_s, tms, 1, False)
    o_r_s, s_new = _retention(qr_s, kr_s, vr_s, state_ret[0], DB, L, L)
    o_a_s, nk, nv = _sample_attention(qa_s, ka_s, va_s, cache_win_k[0].reshape(DB, WB, ATT_W),
                                      cache_win_v[0].reshape(DB, WB, ATT_W), DB, L)

    T_all = Tp + Ts
    zero_cnt = jnp.zeros((1, N_EXPERTS), F32)
    x1_p, h2_p, gt_p, p8_p, p8t_p, meta_p, cnt_p = _merge(
        o_r_p, gr, o_a_p, x_prompt.reshape(Tp, D), mp[2], mp[3], mp[4], beta_ret[0], beta_att[0], w_out_bf,
        norm2_g[0], router_w[0], router_b[0], zero_cnt, tmp, S // tmp)
    x1_s, h2_s, gt_s, p8_s, p8t_s, meta_s, cnt = _merge(
        o_r_s, gr_s, o_a_s[None], x_sample.reshape(Ts, D), msm[2], msm[3], msm[4], beta_ret[0], beta_att[0],
        w_out_bf, norm2_g[0], router_w[0], router_b[0], cnt_p, tms, 1)

    tile = min(TOKEN_TILE, tmp)
    assert tile == min(TOKEN_TILE, tms) and tile == Ts, "both groups need the same token tile"
    tme = min(MOE_TILE, T_all)
    meta = jnp.concatenate([meta_p, meta_s], axis=0)
    blk_exp, blk_new, n_used, fill_start, fill_n, chunk_row, tot, nb = _route(
        cnt, meta, T_all * TOP_K, tme, _slots(tile) // SEG)
    fill_max = 1 << (tme + SEG - 2).bit_length() >> 1
    xs = _dispatch(h2_p, h2_s, jnp.concatenate([p8t_p, p8t_s], axis=1), chunk_row, tot, fill_start, fill_n,
                   nb * tme, tile, fill_max)
    yb = _moe_experts(xs, blk_exp, blk_new, n_used, w_up[0], b_up[0], w_down[0], b_down[0], tme)

    tp = Tp // tile
    y_p = _combine(chunk_row[:tp], tot[:tp], yb, p8_p, gt_p, x1_p, mp[5], fp[0], fp[1], final_norm_g,
                   tile, S // tile)
    y_s = _combine(chunk_row[tp:], tot[tp:], yb, p8_s, gt_s, x1_s, msm[5], fs[0], fs[1], final_norm_g,
                   tile, 1)

    shp = (DB, WB, H_ATT, HD_ATT)
    return (y_p.reshape(B, S, D), y_s.reshape(DB, L, D), s_fin[None], s_new[None],
            wk_p[None], wv_p[None], nk.reshape(shp)[None], nv.reshape(shp)[None])
```

```python
import functools
import math

import numpy as np
import jax
import jax.numpy as jnp
from jax import lax
from jax.experimental import pallas as pl
from jax.experimental.pallas import tpu as pltpu

F32 = jnp.float32
BF16 = jnp.bfloat16
HIGHEST = lax.Precision.HIGHEST

D_MODEL = 1024
PAST_LEN = 16384
H_RET, DK_RET, DV_RET = 4, 64, 128
RET_CHUNK = 256
H_ATT, HD_ATT = 8, 64
DILATED_PAIRS = ((128, 1), (512, 4), (2048, 16))
ATT_BLOCK = 128
MAX_WINDOW = 2048
ROPE_THETA = 10000.0
RET_QK = H_RET * DK_RET
RET_V = H_RET * DV_RET
ATT_W = H_ATT * HD_ATT
IN_WIDTH = 2 * RET_QK + 2 * RET_V + 3 * ATT_W
N_EXPERTS = 32
TOP_K = 4
D_FF = 1024
SWIGLU_LIMIT = 7.0
SWIGLU_ALPHA = 1.702
N_MOD = 6
EPS = 1e-6

LANES = 128
VMEM_LIMIT = 56 * 1024 * 1024
ROW_TILE = 512
MOE_TILE = 512
TOKEN_TILE = 256
N_SLABS = ATT_W // LANES
ROW_CHUNKS = D_MODEL // LANES
SEG = 8
ATT_SPAN = 2048

RET_LOG_DECAY = tuple(math.log(1.0 - 2.0 ** (-5.0 - h)) for h in range(H_RET))


def _params(sem, vmem=VMEM_LIMIT):
    return pltpu.CompilerParams(dimension_semantics=sem, vmem_limit_bytes=vmem)


def _sigmoid(z):
    return 1.0 / (1.0 + jnp.exp(-z))


def _adaln_body(c_ref, w_ref, b_ref, o_ref):
    c = c_ref[...]
    s = c * _sigmoid(c)
    o_ref[...] = jnp.dot(s, w_ref[...], precision=HIGHEST, preferred_element_type=F32) + b_ref[...]


def _adaln(c, w, b):
    R, D = c.shape
    N = w.shape[1]
    tn = 1024
    return pl.pallas_call(
        _adaln_body,
        grid=(N // tn,),
        in_specs=[pl.BlockSpec((R, D), lambda j: (0, 0)),
                  pl.BlockSpec((D, tn), lambda j: (0, j)),
                  pl.BlockSpec((1, tn), lambda j: (0, j))],
        out_specs=pl.BlockSpec((R, tn), lambda j: (0, j)),
        out_shape=jax.ShapeDtypeStruct((R, N), F32),
        compiler_params=_params(("arbitrary",)),
        name="adaln",
    )(c, w, b.reshape(1, N))


def _rope_chunk(xc, cos, sin_signed, first_half):
    partner = jnp.where(first_half, pltpu.roll(xc, 96, 1), pltpu.roll(xc, 32, 1))
    return xc * cos + partner * sin_signed


def _inproj_body(x_ref, sh_ref, sc_ref, g_ref, w_ref, cos_ref, sin_ref, *refs, dilated):
    x = x_ref[...]
    tm = x.shape[0]
    ms = jnp.mean(x * x, axis=-1, keepdims=True)
    h = x * lax.rsqrt(ms + EPS) * g_ref[...]
    h = h * (1.0 + sc_ref[...]) + sh_ref[...]
    p = jnp.dot(h.astype(BF16), w_ref[...], preferred_element_type=F32)
    cos = cos_ref[...]
    sin = sin_ref[...]
    first_half = (lax.broadcasted_iota(jnp.int32, (1, LANES), 1) % HD_ATT) < (HD_ATT // 2)

    def chunk(off, c, rope):
        xc = p[:, off + c * LANES: off + (c + 1) * LANES]
        return _rope_chunk(xc, cos, sin, first_half) if rope else xc

    qr_ref, kr_ref, vr_ref, gr_ref = refs[:4]
    o = 0
    for c in range(RET_QK // LANES):
        qr_ref[:, c * LANES:(c + 1) * LANES] = chunk(o, c, True)
    o += RET_QK
    for c in range(RET_QK // LANES):
        kr_ref[:, c * LANES:(c + 1) * LANES] = chunk(o, c, True) * (DK_RET ** -0.5)
    o += RET_QK
    vr_ref[...] = p[:, o:o + RET_V]; o += RET_V
    gr_ref[...] = p[:, o:o + RET_V]; o += RET_V
    if not dilated:
        qa_ref, ka_ref, va_ref = refs[4:7]
        for ref, rope in ((qa_ref, True), (ka_ref, True), (va_ref, False)):
            for c in range(N_SLABS):
                ref[:, c * LANES:(c + 1) * LANES] = chunk(o, c, rope)
            o += ATT_W
        return
    ka_ref, va_ref = refs[4:6]
    slab_refs = refs[6:15]
    scr, scr4 = refs[15:17]
    n4 = tm // 4
    n16 = tm // 16
    for t, (nat_ref, rope, scale) in enumerate(((None, True, HD_ATT ** -0.5), (ka_ref, True, 1.0),
                                                (va_ref, False, 1.0))):
        s1_ref, s4_ref, s16_ref = slab_refs[t], slab_refs[3 + t], slab_refs[6 + t]
        for c in range(N_SLABS):
            xc = chunk(o, c, rope)
            if nat_ref is not None:
                nat_ref[:, c * LANES:(c + 1) * LANES] = xc
            if scale != 1.0:
                xc = xc * scale
            scr[...] = xc
            s1_ref[c] = xc.astype(BF16)
            for r in range(4):
                part = scr[pl.ds(r, n4, stride=4), :]
                s4_ref[c, r] = part.astype(BF16)
                scr4[r] = part
            for r in range(4):
                for r2 in range(4):
                    s16_ref[c, 4 * r2 + r] = scr4[r, pl.ds(r2, n16, stride=4), :].astype(BF16)
        o += ATT_W


def _mod_spec(mod, tm, tiles_per_group):
    R = mod.shape[1]
    return pl.BlockSpec((None, R, mod.shape[2]), lambda i: (i // tiles_per_group, 0, 0))


def _inproj(x2d, sh, sc, g, w_bf, cos_t, sin_t, tm, tiles_per_group, dilated, B=1):
    T, D = x2d.shape
    pos_tiles = cos_t.shape[0] // tm
    row = lambda w: pl.BlockSpec((tm, w), lambda i: (i, 0))
    tab = pl.BlockSpec((tm, LANES), lambda i: (i % pos_tiles, 0))
    f32 = lambda w: jax.ShapeDtypeStruct((T, w), F32)
    if dilated:
        S = T // B
        tpb = S // tm
        widths = (RET_QK, RET_QK, RET_V, RET_V, ATT_W, ATT_W)
        out_specs = [row(w) for w in widths]
        out_shape = [f32(w) for w in widths]
        for dil in (1, 4, 16):
            for _ in range(3):
                if dil == 1:
                    out_specs.append(pl.BlockSpec((None, N_SLABS, tm, LANES),
                                                  lambda i: (i // tpb, 0, i % tpb, 0)))
                    out_shape.append(jax.ShapeDtypeStruct((B, N_SLABS, S, LANES), BF16))
                else:
                    out_specs.append(pl.BlockSpec((None, N_SLABS, dil, tm // dil, LANES),
                                                  lambda i: (i // tpb, 0, 0, i % tpb, 0)))
                    out_shape.append(jax.ShapeDtypeStruct((B, N_SLABS, dil, S // dil, LANES), BF16))
        scratch = [pltpu.VMEM((tm, LANES), F32), pltpu.VMEM((4, tm // 4, LANES), F32)]
    else:
        widths = (RET_QK, RET_QK, RET_V, RET_V, ATT_W, ATT_W, ATT_W)
        out_specs = [row(w) for w in widths]
        out_shape = [f32(w) for w in widths]
        scratch = []
    return pl.pallas_call(
        functools.partial(_inproj_body, dilated=dilated),
        grid=(T // tm,),
        in_specs=[row(D), _mod_spec(sh, tm, tiles_per_group), _mod_spec(sc, tm, tiles_per_group),
                  pl.BlockSpec((1, D), lambda i: (0, 0)),
                  pl.BlockSpec((D, IN_WIDTH), lambda i: (0, 0)), tab, tab],
        out_specs=out_specs, out_shape=out_shape, scratch_shapes=scratch,
        compiler_params=_params(("arbitrary",)),
        name="inproj",
    )(x2d, sh, sc, g.reshape(1, D), w_bf, cos_t, sin_t)


def _rope_tables(pos):
    half = HD_ATT // 2
    inv = ROPE_THETA ** (-jnp.arange(half, dtype=F32) / half)
    ang = pos.astype(F32)[:, None] * inv[None, :]
    cos = jnp.cos(ang)
    sin = jnp.sin(ang)
    cos_t = jnp.tile(cos, (1, LANES // half))
    sin_t = jnp.tile(jnp.concatenate([-sin, sin], axis=1), (1, LANES // HD_ATT))
    return cos_t, sin_t


def _retention_body(q_ref, k_ref, v_ref, s0_ref, o_ref, sfin_ref, s_scr, *, L):
    c = pl.program_id(1)

    @pl.when(c == 0)
    def _():
        s_scr[...] = s0_ref[...]

    ii = lax.broadcasted_iota(jnp.int32, (L, L), 0)
    jj = lax.broadcasted_iota(jnp.int32, (L, L), 1)
    diff = (ii - jj).astype(F32)
    causal = diff >= 0
    row = lax.broadcasted_iota(jnp.int32, (L, 1), 0).astype(F32)
    for h in range(H_RET):
        lg = RET_LOG_DECAY[h]
        decay = jnp.where(causal, jnp.exp(jnp.where(causal, diff, 0.0) * lg), 0.0)
        q = q_ref[:, h * DK_RET:(h + 1) * DK_RET]
        k = k_ref[:, h * DK_RET:(h + 1) * DK_RET]
        v = v_ref[:, h * DV_RET:(h + 1) * DV_RET]
        qb, vb = q.astype(BF16), v.astype(BF16)
        scores = lax.dot_general(qb, k.astype(BF16), (((1,), (1,)), ((), ())),
                                 preferred_element_type=F32) * decay
        S = s_scr[h]
        o = jnp.dot(scores.astype(BF16), vb, preferred_element_type=F32)
        o = o + jnp.dot(qb, S.astype(BF16), preferred_element_type=F32) * jnp.exp((row + 1.0) * lg)
        k_dec = k * jnp.exp((L - 1.0 - row) * lg)
        s_scr[h] = math.exp(L * lg) * S + lax.dot_general(
            k_dec.astype(BF16), vb, (((0,), (0,)), ((), ())), preferred_element_type=F32)
        o_ref[:, h * DV_RET:(h + 1) * DV_RET] = o

    @pl.when(c == pl.num_programs(1) - 1)
    def _():
        sfin_ref[...] = s_scr[...]


def _retention(q2d, k2d, v2d, s0, B, L_seq, L):
    nc = L_seq // L
    row = lambda w: pl.BlockSpec((L, w), lambda b, c: (b * nc + c, 0))
    st = pl.BlockSpec((None, H_RET, DK_RET, DV_RET), lambda b, c: (b, 0, 0, 0))
    return pl.pallas_call(
        functools.partial(_retention_body, L=L),
        grid=(B, nc),
        in_specs=[row(RET_QK), row(RET_QK), row(RET_V), st],
        out_specs=[row(RET_V), st],
        out_shape=[jax.ShapeDtypeStruct((B * L_seq, RET_V), F32),
                   jax.ShapeDtypeStruct((B, H_RET, DK_RET, DV_RET), F32)],
        scratch_shapes=[pltpu.VMEM((H_RET, DK_RET, DV_RET), F32)],
        compiler_params=_params(("arbitrary", "arbitrary")),
        name="retention",
    )(q2d, k2d, v2d, s0)


def _band_attention(q, kcat, vcat, valid, low):
    nt = (((1,), (1,)), ((), ()))
    zero = jnp.zeros_like(q)
    outs = []
    for hh in range(2):
        qh = jnp.where(low if hh == 0 else ~low, q, zero)
        s = lax.dot_general(qh, kcat, nt, preferred_element_type=F32)
        s = jnp.where(valid, s, -jnp.inf)
        m = jnp.max(s, axis=-1, keepdims=True)
        p = jnp.exp(s - m)
        l = jnp.sum(p, axis=-1, keepdims=True)
        outs.append((jnp.dot(p.astype(BF16), vcat, preferred_element_type=F32), m, l))
    (a0, m0, l0), (a1, m1, l1) = outs
    shape = a0.shape
    return (jnp.where(low, a0, a1), jnp.where(low, jnp.broadcast_to(m0, shape), jnp.broadcast_to(m1, shape)),
            jnp.where(low, jnp.broadcast_to(l0, shape), jnp.broadcast_to(l1, shape)))


def _dilated_body(q1, k1, k1p, v1, v1p, q4, k4, k4p, v4, v4p, q16, k16, k16p, v16, v16p,
                  o_ref, kf1, vf1, kf4, vf4, acc_s, m_s, l_s):
    j = pl.program_id(2)
    nq = ATT_BLOCK
    n1 = ATT_SPAN // nq
    GROUP = 4
    kf1[0:nq] = k1p[...]
    kf1[nq:] = k1[...]
    vf1[0:nq] = v1p[...]
    vf1[nq:] = v1[...]
    for r in range(4):
        kf4[r, 0:nq] = k4p[r]
        kf4[r, nq:] = k4[r]
        vf4[r, 0:nq] = v4p[r]
        vf4[r, nq:] = v4[r]
    qi = lax.broadcasted_iota(jnp.int32, (nq, 2 * nq), 0) + nq
    ki = lax.broadcasted_iota(jnp.int32, (nq, 2 * nq), 1)
    dist = qi - ki
    band = (dist >= 0) & (dist <= nq)
    in_cur = ki >= nq
    low = lax.broadcasted_iota(jnp.int32, (1, LANES), 1) < HD_ATT

    def mask(has_prev):
        return band & (has_prev | in_cur)

    def merge(rows, a, m, l):
        m_old, l_old = m_s[rows, :], l_s[rows, :]
        m_new = jnp.maximum(m_old, m)
        w_old, w_new = jnp.exp(m_old - m_new), jnp.exp(m - m_new)
        return acc_s[rows, :] * w_old + a * w_new, m_new, l_old * w_old + l * w_new

    def dil1(g, carry):
        res = []
        for u in range(GROUP):
            qb = g * GROUP + u
            r0 = pl.multiple_of(qb * nq, nq)
            res.append((pl.ds(r0, nq), _band_attention(
                q1[pl.ds(r0, nq), :], kf1[pl.ds(r0, 2 * nq), :], vf1[pl.ds(r0, 2 * nq), :],
                mask((j > 0) | (qb > 0)), low)))
        for rows, (a, m, l) in res:
            acc_s[rows, :] = a
            m_s[rows, :] = m
            l_s[rows, :] = l
        return carry

    lax.fori_loop(0, n1 // GROUP, dil1, 0)

    def dil4(qb, carry):
        r0 = pl.multiple_of(qb * nq, nq)
        res = []
        for r in range(4):
            res.append((pl.ds(qb * (4 * nq) + r, nq, stride=4), _band_attention(
                q4[r, pl.ds(r0, nq), :], kf4[r, pl.ds(r0, 2 * nq), :], vf4[r, pl.ds(r0, 2 * nq), :],
                mask((j > 0) | (qb > 0)), low)))
        for rows, (a, m, l) in res:
            a, m, l = merge(rows, a, m, l)
            acc_s[rows, :] = a
            m_s[rows, :] = m
            l_s[rows, :] = l
        return carry

    lax.fori_loop(0, ATT_SPAN // 4 // nq, dil4, 0)

    def dil16(g, carry):
        res = []
        for u in range(GROUP):
            r = g * GROUP + u
            kcat = jnp.concatenate([k16p[r], k16[r]], axis=0)
            vcat = jnp.concatenate([v16p[r], v16[r]], axis=0)
            res.append((pl.ds(r, nq, stride=16), _band_attention(q16[r], kcat, vcat, mask(j > 0), low)))
        for rows, (a, m, l) in res:
            a, m, l = merge(rows, a, m, l)
            o_ref[rows, :] = a / l
        return carry

    lax.fori_loop(0, 16 // GROUP, dil16, 0)


def _dilated_prompt(slabs, B, S):
    q1, k1, v1, q4, k4, v4, q16, k16, v16 = slabs
    assert S % ATT_SPAN == 0 and DILATED_PAIRS == ((128, 1), (512, 4), (2048, 16))
    nq = ATT_BLOCK
    prev = lambda j, per: jnp.maximum(j * per - 1, 0)
    s1 = pl.BlockSpec((None, None, ATT_SPAN, LANES), lambda b, c, j: (b, c, j, 0))
    s1p = pl.BlockSpec((None, None, nq, LANES), lambda b, c, j: (b, c, prev(j, ATT_SPAN // nq), 0))
    s4 = pl.BlockSpec((None, None, 4, ATT_SPAN // 4, LANES), lambda b, c, j: (b, c, 0, j, 0))
    s4p = pl.BlockSpec((None, None, 4, nq, LANES), lambda b, c, j: (b, c, 0, prev(j, ATT_SPAN // 4 // nq), 0))
    s16 = pl.BlockSpec((None, None, 16, nq, LANES), lambda b, c, j: (b, c, 0, j, 0))
    s16p = pl.BlockSpec((None, None, 16, nq, LANES), lambda b, c, j: (b, c, 0, prev(j, 1), 0))
    return pl.pallas_call(
        _dilated_body,
        grid=(B, N_SLABS, S // ATT_SPAN),
        in_specs=[s1, s1, s1p, s1, s1p, s4, s4, s4p, s4, s4p, s16, s16, s16p, s16, s16p],
        out_specs=pl.BlockSpec((None, None, ATT_SPAN, LANES), lambda b, c, j: (b, c, j, 0)),
        out_shape=jax.ShapeDtypeStruct((B, N_SLABS, S, LANES), F32),
        scratch_shapes=[pltpu.VMEM((ATT_SPAN + nq, LANES), BF16), pltpu.VMEM((ATT_SPAN + nq, LANES), BF16),
                        pltpu.VMEM((4, ATT_SPAN // 4 + nq, LANES), BF16),
                        pltpu.VMEM((4, ATT_SPAN // 4 + nq, LANES), BF16),
                        pltpu.VMEM((ATT_SPAN, LANES), F32), pltpu.VMEM((ATT_SPAN, LANES), F32),
                        pltpu.VMEM((ATT_SPAN, LANES), F32)],
        compiler_params=_params(("arbitrary", "arbitrary", "arbitrary")),
        name="dilated_attention",
    )(q1, k1, k1, v1, v1, q4, k4, k4, v4, v4, q16, k16, k16, v16, v16)


def _sample_multiplicity(L, WB):
    pad = LANES
    cnt = np.zeros((L, WB + pad), np.float32)
    for l in range(L):
        for window, dil in DILATED_PAIRS:
            for j in range(window // dil + 1):
                idx = WB + l - dil * j
                if idx >= 0:
                    cnt[l, idx] += 1.0
    cnt = np.tile(cnt, (H_ATT, 1))
    return cnt[:, :WB], cnt[:, WB:]


def _sample_att_body(q_ref, kn_ref, vn_ref, bk_ref, bv_ref, cb_ref, cn_ref,
                     o_ref, nk_ref, nv_ref, *, L, WB):
    q = q_ref[...]
    lane_head = lax.broadcasted_iota(jnp.int32, (1, ATT_W), 1) // HD_ATT
    qexp = jnp.concatenate([jnp.where(lane_head == h, q, 0.0) for h in range(H_ATT)], axis=0).astype(BF16)
    zpad = jnp.zeros((LANES - L, ATT_W), F32)
    kn = jnp.concatenate([kn_ref[...], zpad], axis=0).astype(BF16)
    vn = jnp.concatenate([vn_ref[...], zpad], axis=0).astype(BF16)
    bk = bk_ref[...]
    bv = bv_ref[...]
    nt = (((1,), (1,)), ((), ()))
    scale = HD_ATT ** -0.5
    sb = lax.dot_general(qexp, bk.astype(BF16), nt, preferred_element_type=F32) * scale
    sn = lax.dot_general(qexp, kn, nt, preferred_element_type=F32) * scale
    cb = cb_ref[...]
    cn = cn_ref[...]
    sb = jnp.where(cb > 0, sb, -jnp.inf)
    sn = jnp.where(cn > 0, sn, -jnp.inf)
    m = jnp.maximum(jnp.max(sb, axis=-1, keepdims=True), jnp.max(sn, axis=-1, keepdims=True))
    pb = cb * jnp.exp(sb - m)
    pn = cn * jnp.exp(sn - m)
    l = jnp.sum(pb, axis=-1, keepdims=True) + jnp.sum(pn, axis=-1, keepdims=True)
    o_all = (jnp.dot(pb.astype(BF16), bv.astype(BF16), preferred_element_type=F32)
             + jnp.dot(pn.astype(BF16), vn, preferred_element_type=F32)) / l
    o = jnp.zeros((L, ATT_W), F32)
    for h in range(H_ATT):
        o = o + jnp.where(lane_head == h, o_all[h * L:(h + 1) * L], 0.0)
    for c in range(N_SLABS):
        o_ref[c] = o[:, c * LANES:(c + 1) * LANES]
    nk_ref[0:WB - L, :] = bk[L:WB]
    nk_ref[WB - L:WB, :] = kn_ref[...]
    nv_ref[0:WB - L, :] = bv[L:WB]
    nv_ref[WB - L:WB, :] = vn_ref[...]


def _sample_attention(q2d, k2d, v2d, buf_k, buf_v, DB, L):
    WB = buf_k.shape[1]
    cb, cn = _sample_multiplicity(L, WB)
    row = pl.BlockSpec((L, ATT_W), lambda b: (b, 0))
    buf = pl.BlockSpec((None, WB, ATT_W), lambda b: (b, 0, 0))
    const = lambda a: pl.BlockSpec(a.shape, lambda b: (0, 0))
    return pl.pallas_call(
        functools.partial(_sample_att_body, L=L, WB=WB),
        grid=(DB,),
        in_specs=[row, row, row, buf, buf, const(cb), const(cn)],
        out_specs=[pl.BlockSpec((N_SLABS, L, LANES), lambda b: (0, b, 0)), buf, buf],
        out_shape=[jax.ShapeDtypeStruct((N_SLABS, DB * L, LANES), F32),
                   jax.ShapeDtypeStruct((DB, WB, ATT_W), F32),
                   jax.ShapeDtypeStruct((DB, WB, ATT_W), F32)],
        compiler_params=_params(("arbitrary",)),
        name="sample_attention",
    )(q2d, k2d, v2d, buf_k, buf_v, jnp.asarray(cb), jnp.asarray(cn))


def _group_mean(sq, width):
    if width == LANES:
        return jnp.mean(sq, axis=-1, keepdims=True)
    lane = lax.broadcasted_iota(jnp.int32, (1, LANES), 1)
    low = lane < width
    s_lo = jnp.sum(jnp.where(low, sq, 0.0), axis=-1, keepdims=True)
    s_hi = jnp.sum(jnp.where(low, 0.0, sq), axis=-1, keepdims=True)
    return jnp.where(low, s_lo, s_hi) * (1.0 / width)


def _merge_body(or_ref, gr_ref, oa0, oa1, oa2, oa3, x_ref, g1_ref, sh_ref, sc_ref, br_ref, ba_ref, wo_ref,
                n2_ref, rw_ref, rb_ref, cin_ref,
                x1_ref, h2_ref, gt_ref, p8_ref, p8t_ref, meta_ref, cnt_ref, cnt_scr):
    i = pl.program_id(0)

    @pl.when(i == 0)
    def _():
        cnt_scr[...] = cin_ref[...]

    parts = []
    for c in range(RET_V // LANES):
        cs = slice(c * LANES, (c + 1) * LANES)
        o = or_ref[:, cs]
        o = o * lax.rsqrt(_group_mean(o * o, DV_RET) + EPS) * br_ref[:, cs]
        g = gr_ref[:, cs]
        parts.append((o * (g * _sigmoid(g))).astype(BF16))
    for c, oa in enumerate((oa0, oa1, oa2, oa3)):
        cs = slice(c * LANES, (c + 1) * LANES)
        o = oa[...]
        o = o * lax.rsqrt(_group_mean(o * o, HD_ATT) + EPS) * ba_ref[:, cs]
        parts.append(o.astype(BF16))
    mix = jnp.concatenate(parts, axis=1)
    y = jnp.dot(mix, wo_ref[...], preferred_element_type=F32)
    x1 = x_ref[...] + g1_ref[...] * y
    x1_ref[...] = x1
    ms = jnp.mean(x1 * x1, axis=-1, keepdims=True)
    h2 = x1 * lax.rsqrt(ms + EPS) * n2_ref[...]
    h2 = h2 * (1.0 + sc_ref[...]) + sh_ref[...]
    rw = rw_ref[...]
    h_hi = h2.astype(BF16)
    h2_ref[...] = h_hi
    h_lo = (h2 - h_hi.astype(F32)).astype(BF16)
    w_hi = rw.astype(BF16)
    w_lo = (rw - w_hi.astype(F32)).astype(BF16)
    logits = (jnp.dot(h_hi, w_hi, preferred_element_type=F32) + jnp.dot(h_lo, w_hi, preferred_element_type=F32)
              + jnp.dot(h_hi, w_lo, preferred_element_type=F32)) + rb_ref[...]
    tm = logits.shape[0]
    lane = lax.broadcasted_iota(jnp.int32, logits.shape, 1).astype(F32)
    vals, idxs = [], []
    for _ in range(TOP_K):
        m = jnp.max(logits, axis=-1, keepdims=True)
        idx = jnp.min(jnp.where(logits == m, lane, float(N_EXPERTS)), axis=-1, keepdims=True)
        vals.append(m)
        idxs.append(idx)
        logits = jnp.where(lane == idx, -jnp.inf, logits)
    e = [jnp.exp(v - vals[0]) for v in vals]
    tot = e[0] + e[1] + e[2] + e[3]
    for k in range(TOP_K):
        gt_ref[:, k:k + 1] = e[k] / tot
    member = jnp.zeros(logits.shape, F32)
    for k in range(TOP_K):
        member = member + jnp.where(lane == idxs[k], 1.0, 0.0)
    sub = min(TOKEN_TILE, tm)
    ti = lax.broadcasted_iota(jnp.int32, (tm, tm), 0)
    tj = lax.broadcasted_iota(jnp.int32, (tm, tm), 1)
    before = jnp.where((tj < ti) & (tj // sub == ti // sub), 1.0, 0.0).astype(BF16)
    rin = jnp.dot(before, member.astype(BF16), preferred_element_type=F32)
    earlier = jnp.where(lax.broadcasted_iota(jnp.int32, (N_EXPERTS, N_EXPERTS), 0)
                        < lax.broadcasted_iota(jnp.int32, (N_EXPERTS, N_EXPERTS), 1), 1.0, 0.0)
    row_sub = lax.broadcasted_iota(jnp.int32, (tm, 1), 0) // sub
    carry = cnt_scr[...]
    off8 = jnp.zeros(logits.shape, F32)
    for s in range(tm // sub):
        in_sub = row_sub == s
        n = jnp.sum(jnp.where(in_sub, member, 0.0), axis=0, keepdims=True)
        n8 = jnp.floor((n + 7.0) * 0.125) * 8.0
        o8 = jnp.dot(jnp.broadcast_to(n8, (8, N_EXPERTS)), earlier, precision=HIGHEST,
                     preferred_element_type=F32)
        o8 = jnp.max(o8, axis=0, keepdims=True)
        off8 = off8 + jnp.where(in_sub, o8, 0.0)
        meta_ref[s, 0:1, :] = n
        meta_ref[s, 1:2, :] = carry
        carry = carry + n
    lane128 = lax.broadcasted_iota(jnp.int32, (tm, LANES), 1)
    slots_k = jnp.zeros((tm, LANES), F32)
    for k in range(TOP_K):
        pos = jnp.sum(jnp.where(lane == idxs[k], rin + off8, 0.0), axis=-1, keepdims=True)
        p8_ref[:, k:k + 1] = pos.astype(jnp.int32)
        slots_k = slots_k + jnp.where(lane128 == k, pos, 0.0)
    p8t_ref[...] = slots_k.T[0:8, :].astype(jnp.int32)
    cnt_scr[...] = carry
    cnt_ref[...] = carry


def _merge(o_r, g_r, o_a, x2d, g1, sh2, sc2, beta_ret, beta_att, wo_bf, n2, rw, rb, cnt_in, tm,
           tiles_per_group):
    T, D = x2d.shape
    tpg = o_a.shape[2] // tm
    sub = min(TOKEN_TILE, tm)
    row = lambda w: pl.BlockSpec((tm, w), lambda i: (i, 0))
    const = lambda r, c: pl.BlockSpec((r, c), lambda i: (0, 0))
    ms = lambda m: _mod_spec(m, tm, tiles_per_group)
    slab = lambda c: pl.BlockSpec((None, None, tm, LANES), lambda i: (i // tpg, c, i % tpg, 0))
    ins = [o_r, g_r, o_a, o_a, o_a, o_a, x2d, g1, sh2, sc2, beta_ret.reshape(1, RET_V),
           beta_att.reshape(1, ATT_W), wo_bf, n2.reshape(1, D), rw, rb.reshape(1, N_EXPERTS), cnt_in]
    in_specs = [row(RET_V), row(RET_V), slab(0), slab(1), slab(2), slab(3), row(D), ms(g1), ms(sh2), ms(sc2),
                const(1, RET_V), const(1, ATT_W), const(D, D), const(1, D),
                const(D, N_EXPERTS), const(1, N_EXPERTS), const(1, N_EXPERTS)]
    return pl.pallas_call(
        _merge_body,
        grid=(T // tm,),
        in_specs=in_specs,
        out_specs=[row(D), row(D), row(TOP_K), row(TOP_K), pl.BlockSpec((8, tm), lambda i: (0, i)),
                   pl.BlockSpec((tm // sub, 2, N_EXPERTS), lambda i: (i, 0, 0)), const(1, N_EXPERTS)],
        out_shape=[jax.ShapeDtypeStruct((T, D), F32), jax.ShapeDtypeStruct((T, D), BF16),
                   jax.ShapeDtypeStruct((T, TOP_K), F32), jax.ShapeDtypeStruct((T, TOP_K), jnp.int32),
                   jax.ShapeDtypeStruct((8, T), jnp.int32),
                   jax.ShapeDtypeStruct((T // sub, 2, N_EXPERTS), F32),
                   jax.ShapeDtypeStruct((1, N_EXPERTS), F32)],
        scratch_shapes=[pltpu.VMEM((1, N_EXPERTS), F32)],
        compiler_params=_params(("arbitrary",)),
        name="merge_router",
    )(*ins)


def _slots(tile):
    return -(-(tile * TOP_K + N_EXPERTS * SEG) // LANES) * LANES


def _route(counts, meta, n_assign, tm, max_chunks):
    nb = -(-(n_assign + N_EXPERTS * (tm - 1 + SEG - 1)) // tm)
    experts = jnp.arange(N_EXPERTS, dtype=jnp.int32)
    counts = counts.reshape(N_EXPERTS).astype(jnp.int32)
    padded = (counts + (SEG - 1) + tm - 1) // tm * tm
    pad_end = jnp.cumsum(padded)
    pad_start = pad_end - padded
    blk0 = jnp.arange(nb + 1, dtype=jnp.int32) * tm
    blk_exp = jnp.minimum(jnp.sum((pad_end[None, :] <= blk0[:, None]).astype(jnp.int32), axis=1),
                          N_EXPERTS - 1).astype(jnp.int32)
    n_used = (pad_end[-1:] // tm).astype(jnp.int32)
    blk_new = jnp.concatenate([jnp.ones((1,), jnp.int32),
                               (blk_exp[1:] != blk_exp[:-1]).astype(jnp.int32)])
    fill_start = jnp.concatenate([pad_start + counts, pad_end[-1:]]).astype(jnp.int32)
    fill_n = jnp.concatenate([padded - counts, nb * tm - pad_end[-1:]]).astype(jnp.int32)
    n = meta[:, 0, :].astype(jnp.int32)
    start = pad_start[None, :] + meta[:, 1, :].astype(jnp.int32)
    nseg = (n + SEG - 1) // SEG
    cum = jnp.cumsum(nseg, axis=1)
    m = jnp.arange(max_chunks, dtype=jnp.int32)
    e_of = jnp.sum((cum[:, None, :] <= m[None, :, None]).astype(jnp.int32), axis=-1)
    onehot = e_of[:, :, None] == experts[None, None, :]
    first = jnp.sum(jnp.where(onehot, (cum - nseg)[:, None, :], 0), axis=-1)
    st = jnp.sum(jnp.where(onehot, start[:, None, :], 0), axis=-1)
    chunk_row = (st + SEG * (m[None, :] - first)).astype(jnp.int32)
    return blk_exp, blk_new, n_used, fill_start, fill_n, chunk_row, cum[:, -1].astype(jnp.int32), nb


def _wait_copies(count, sem, src, dst, rows_per_copy, max_count):
    sz = 1 << (max_count.bit_length() - 1)
    while sz >= 1:
        @pl.when((count & sz) != 0)
        def _(sz=sz):
            pltpu.make_async_copy(src.at[pl.ds(0, sz * rows_per_copy)], dst.at[pl.ds(0, sz * rows_per_copy)],
                                  sem).wait()
        sz //= 2


def _dispatch_body(rows_ref, tot_ref, fs_ref, fn_ref, p8t_ref, hp_ref, hs_ref, xs_hbm,
                   sbuf, zbuf, sem, zsem, *, tm, slots, max_chunks, fill_max, prompt_tiles):
    i = pl.program_id(0)
    slot = i % 2
    seg_rows = SEG * ROW_CHUNKS

    h = jnp.where(i < prompt_tiles, hp_ref[...], hs_ref[...])
    s_iota = lax.broadcasted_iota(jnp.int32, (slots, tm), 0)
    p = jnp.zeros((slots, tm), F32)
    for k in range(TOP_K):
        p = p + jnp.where(s_iota == p8t_ref[k:k + 1, :], 1.0, 0.0)
    srt = jnp.dot(p.astype(BF16), h, preferred_element_type=F32)
    for c in range(ROW_CHUNKS):
        sbuf[slot, pl.ds(c, slots, stride=ROW_CHUNKS), :] = srt[:, c * LANES:(c + 1) * LANES]

    @pl.when(i > 0)
    def _():
        _wait_copies(tot_ref[i - 1], sem.at[1 - slot], sbuf.at[1 - slot], xs_hbm, seg_rows, max_chunks)

    def one(m, carry):
        row = pl.multiple_of(rows_ref[i * max_chunks + m] * ROW_CHUNKS, ROW_CHUNKS)
        src = pl.multiple_of(m * seg_rows, seg_rows)
        pltpu.make_async_copy(sbuf.at[slot, pl.ds(src, seg_rows)], xs_hbm.at[pl.ds(row, seg_rows)],
                              sem.at[slot]).start()
        return carry

    lax.fori_loop(0, tot_ref[i], one, 0)

    @pl.when(i == pl.num_programs(0) - 1)
    def _():
        _wait_copies(tot_ref[i], sem.at[slot], sbuf.at[slot], xs_hbm, seg_rows, max_chunks)
        zbuf[...] = jnp.zeros(zbuf.shape, F32)

        def fill(e, wait):
            start = fs_ref[e]
            n = fn_ref[e]
            sz = fill_max
            while sz >= 1:
                @pl.when((n & sz) != 0)
                def _(start=start, sz=sz):
                    cp = pltpu.make_async_copy(
                        zbuf.at[pl.ds(0, sz * ROW_CHUNKS)],
                        xs_hbm.at[pl.ds(pl.multiple_of(start * ROW_CHUNKS, ROW_CHUNKS), sz * ROW_CHUNKS)], zsem)
                    cp.wait() if wait else cp.start()
                start = start + (n & sz)
                sz //= 2

        def fill_all(wait):
            def body(e, carry):
                fill(e, wait)
                return carry
            lax.fori_loop(0, N_EXPERTS, body, 0)

            def tail(c, carry):
                row = pl.multiple_of((fs_ref[N_EXPERTS] + c * fill_max) * ROW_CHUNKS, ROW_CHUNKS)
                cp = pltpu.make_async_copy(zbuf, xs_hbm.at[pl.ds(row, fill_max * ROW_CHUNKS)], zsem)
                cp.wait() if wait else cp.start()
                return carry
            lax.fori_loop(0, fn_ref[N_EXPERTS] // fill_max, tail, 0)

        fill_all(False)
        fill_all(True)


def _dispatch(h2_p, h2_s, p8t, chunk_row, tot, fill_start, fill_n, R, tm, fill_max):
    Tp, D = h2_p.shape
    Ts = h2_s.shape[0]
    np_, ns_ = Tp // tm, Ts // tm
    slots = _slots(tm)
    max_chunks = slots // SEG
    grid_spec = pltpu.PrefetchScalarGridSpec(
        num_scalar_prefetch=4,
        grid=(np_ + ns_,),
        in_specs=[pl.BlockSpec((8, tm), lambda i, *_: (0, i)),
                  pl.BlockSpec((tm, D), lambda i, *_: (jnp.minimum(i, np_ - 1), 0)),
                  pl.BlockSpec((tm, D), lambda i, *_: (jnp.maximum(i - np_, 0), 0))],
        out_specs=pl.BlockSpec(memory_space=pl.ANY),
        scratch_shapes=[pltpu.VMEM((2, slots * ROW_CHUNKS, LANES), F32),
                        pltpu.VMEM((fill_max * ROW_CHUNKS, LANES), F32),
                        pltpu.SemaphoreType.DMA((2,)), pltpu.SemaphoreType.DMA(())],
    )
    return pl.pallas_call(
        functools.partial(_dispatch_body, tm=tm, slots=slots, max_chunks=max_chunks, fill_max=fill_max,
                          prompt_tiles=np_),
        grid_spec=grid_spec,
        out_shape=jax.ShapeDtypeStruct((R * ROW_CHUNKS, LANES), F32),
        compiler_params=_params(("arbitrary",)),
        name="moe_dispatch",
    )(chunk_row.reshape(-1), tot, fill_start, fill_n, p8t, h2_p, h2_s)


def _moe_body(nused_ref, bexp_ref, new_ref, x_ref, wu_ref, bu_ref, wd_ref, bd_ref, y_ref, wu_bf, wd_bf):
    i = pl.program_id(0)

    @pl.when(new_ref[i] != 0)
    def _():
        wu_bf[...] = wu_ref[...].astype(BF16)
        wd_bf[...] = wd_ref[...].astype(BF16)

    @pl.when(i < nused_ref[0])
    def _():
        tm = x_ref.shape[0] // ROW_CHUNKS
        x = jnp.concatenate([x_ref[pl.ds(c, tm, stride=ROW_CHUNKS), :].astype(BF16)
                             for c in range(ROW_CHUNKS)], axis=1)
        u = jnp.dot(x, wu_bf[...], preferred_element_type=F32) + bu_ref[...]
        x_glu = jnp.minimum(u[:, :D_FF], SWIGLU_LIMIT)
        x_lin = jnp.clip(u[:, D_FF:], -SWIGLU_LIMIT, SWIGLU_LIMIT)
        act = x_glu * _sigmoid(SWIGLU_ALPHA * x_glu) * (x_lin + 1.0)
        y = jnp.dot(act.astype(BF16), wd_bf[...], preferred_element_type=F32) + bd_ref[...]
        for c in range(ROW_CHUNKS):
            y_ref[pl.ds(c, tm, stride=ROW_CHUNKS), :] = y[:, c * LANES:(c + 1) * LANES]

    @pl.when(i >= nused_ref[0])
    def _():
        y_ref[...] = jnp.zeros(y_ref.shape, F32)


def _moe_experts(xs, blk_exp, blk_new, n_used, w_up, bu, w_down, bd, tm):
    R, D = xs.shape[0] // ROW_CHUNKS, D_MODEL
    blk = lambda i, nu, be, nw: (jnp.minimum(i, nu[0] - 1), 0)
    wspec = lambda r, c: pl.BlockSpec((None, r, c), lambda i, nu, be, nw: (be[i], 0, 0))
    grid_spec = pltpu.PrefetchScalarGridSpec(
        num_scalar_prefetch=3,
        grid=(R // tm + 1,),
        in_specs=[pl.BlockSpec((tm * ROW_CHUNKS, LANES), blk), wspec(D, 2 * D_FF), wspec(1, 2 * D_FF),
                  wspec(D_FF, D), wspec(1, D)],
        out_specs=pl.BlockSpec((tm * ROW_CHUNKS, LANES), lambda i, nu, be, nw: (i, 0)),
        scratch_shapes=[pltpu.VMEM((D, 2 * D_FF), BF16), pltpu.VMEM((D_FF, D), BF16)],
    )
    return pl.pallas_call(
        _moe_body,
        grid_spec=grid_spec,
        out_shape=jax.ShapeDtypeStruct(((R + tm) * ROW_CHUNKS, LANES), F32),
        compiler_params=_params(("arbitrary",)),
        name="moe_experts",
    )(n_used, blk_exp, blk_new, xs, w_up, bu.reshape(N_EXPERTS, 1, 2 * D_FF), w_down,
      bd.reshape(N_EXPERTS, 1, D))


def _combine_body(rows_ref, tot_ref, yb_hbm, p8_ref, gate_ref, x1_ref, g2_ref, shf_ref, scf_ref, gf_ref,
                  out_ref, sbuf, sem, *, tm, slots, max_chunks):
    i = pl.program_id(0)
    nsteps = pl.num_programs(0)
    seg_rows = SEG * ROW_CHUNKS

    @pl.when(i == 0)
    def _():
        sbuf[...] = jnp.zeros(sbuf.shape, F32)

    def start_runs(tile, slot):
        def one(m, carry):
            src = pl.multiple_of(rows_ref[tile * max_chunks + m] * ROW_CHUNKS, ROW_CHUNKS)
            dst = pl.multiple_of(m * seg_rows, seg_rows)
            pltpu.make_async_copy(yb_hbm.at[pl.ds(src, seg_rows)], sbuf.at[slot, pl.ds(dst, seg_rows)],
                                  sem.at[slot]).start()
            return carry
        lax.fori_loop(0, tot_ref[tile], one, 0)

    @pl.when(i == 0)
    def _():
        start_runs(0, 0)

    @pl.when(i + 1 < nsteps)
    def _():
        start_runs(i + 1, (i + 1) % 2)

    slot = i % 2
    _wait_copies(tot_ref[i], sem.at[slot], yb_hbm, sbuf.at[slot], seg_rows, max_chunks)
    pos = p8_ref[...]
    gates = gate_ref[...]
    s_iota = lax.broadcasted_iota(jnp.int32, (tm, slots), 1)
    g = jnp.zeros((tm, slots), F32)
    for k in range(TOP_K):
        g = g + jnp.where(s_iota == pos[:, k:k + 1], gates[:, k:k + 1], 0.0)
    gb = g.astype(BF16)
    ys = []
    for c in range(ROW_CHUNKS):
        chunk = sbuf[slot, pl.ds(c, slots, stride=ROW_CHUNKS), :]
        ys.append(jnp.dot(gb, chunk.astype(BF16), preferred_element_type=F32))
    y = jnp.concatenate(ys, axis=1)
    x2 = x1_ref[...] + g2_ref[...] * y
    ms = jnp.mean(x2 * x2, axis=-1, keepdims=True)
    h = x2 * lax.rsqrt(ms + EPS) * gf_ref[...]
    out_ref[...] = h * (1.0 + scf_ref[...]) + shf_ref[...]


def _combine(chunk_row, tot, yb, pos8, gates, x1, g2, shf, scf, gf, tm, tiles_per_group):
    T, D = x1.shape
    nsteps = T // tm
    slots = _slots(tm)
    max_chunks = slots // SEG
    row = lambda w: pl.BlockSpec((tm, w), lambda i, rows, tot: (i, 0))
    ms = lambda m: pl.BlockSpec((None, m.shape[1], m.shape[2]),
                                lambda i, rows, tot: (i // tiles_per_group, 0, 0))
    grid_spec = pltpu.PrefetchScalarGridSpec(
        num_scalar_prefetch=2,
        grid=(nsteps,),
        in_specs=[pl.BlockSpec(memory_space=pl.ANY), row(TOP_K), row(TOP_K), row(D), ms(g2), ms(shf), ms(scf),
                  pl.BlockSpec((1, D), lambda i, rows, tot: (0, 0))],
        out_specs=row(D),
        scratch_shapes=[pltpu.VMEM((2, slots * ROW_CHUNKS, LANES), F32), pltpu.SemaphoreType.DMA((2,))],
    )
    return pl.pallas_call(
        functools.partial(_combine_body, tm=tm, slots=slots, max_chunks=max_chunks),
        grid_spec=grid_spec,
        out_shape=jax.ShapeDtypeStruct((T, D), F32),
        compiler_params=_params(("arbitrary",)),
        name="combine_final",
    )(chunk_row.reshape(-1), tot, yb, pos8, gates, x1, g2, shf, scf, gf.reshape(1, D))


def _group_mods(mod, n, rows_per_group, per_row):
    outs = []
    for j in range(n):
        m = mod[:, j * D_MODEL:(j + 1) * D_MODEL]
        if per_row:
            outs.append(jnp.repeat(m, rows_per_group, axis=0)[None])
        else:
            outs.append(m[:, None, :])
    return outs


def kernel(x_prompt, x_sample, state_ret, cache_win_k, cache_win_v, c_prompt, c_sample,
           ada_w, ada_b, norm1_g, w_in, beta_ret, beta_att, w_out, norm2_g,
           router_w, router_b, w_up, b_up, w_down, b_down,
           final_ada_w, final_ada_b, final_norm_g):
    B, S, D = x_prompt.shape
    DB, L = x_sample.shape[:2]
    WB = cache_win_k.shape[2]
    Tp, Ts = B * S, DB * L
    assert ada_w.shape[0] == 1, "single-layer trunk"
    c_all = jnp.concatenate([c_prompt, c_sample], axis=0)
    c_all = jnp.pad(c_all, ((0, (-c_all.shape[0]) % 8), (0, 0)))
    mods = _adaln(c_all, ada_w[0], ada_b[0])
    fmods = _adaln(c_all, final_ada_w, final_ada_b)
    w_in_bf, w_out_bf = w_in[0].astype(BF16), w_out[0].astype(BF16)

    tmp = min(ROW_TILE, S)
    tms = Ts
    assert Tp % tmp == 0
    mp = _group_mods(mods[:B], N_MOD, S, False)
    msm = _group_mods(mods[B:B + DB], N_MOD, L, True)
    fp = _group_mods(fmods[:B], 2, S, False)
    fs = _group_mods(fmods[B:B + DB], 2, L, True)

    cos_p, sin_p = _rope_tables(jnp.arange(S))
    outs = _inproj(x_prompt.reshape(Tp, D), mp[0], mp[1], norm1_g[0], w_in_bf, cos_p, sin_p,
                   tmp, S // tmp, True, B)
    qr, kr, vr, gr, ka, va = outs[:6]
    o_r_p, s_fin = _retention(qr, kr, vr, jnp.zeros((B, H_RET, DK_RET, DV_RET), F32), B, S, RET_CHUNK)
    o_a_p = _dilated_prompt(outs[6:], B, S)
    keep = min(MAX_WINDOW, S)
    win = lambda t: t.reshape(B, S, ATT_W)[:, S - keep:].reshape(B, keep, H_ATT, HD_ATT)
    wk_p, wv_p = win(ka), win(va)

    cos_s, sin_s = _rope_tables(PAST_LEN + jnp.arange(L))
    cos_s, sin_s = jnp.tile(cos_s, (DB, 1)), jnp.tile(sin_s, (DB, 1))
    qr_s, kr_s, vr_s, gr_s, qa_s, ka_s, va_s = _inproj(
        x_sample.reshape(Ts, D), msm[0], msm[1], norm1_g[0], w_in_bf, cos_s, sin_s, tms, 1, False)
    o_r_s, s_new = _retention(qr_s, kr_s, vr_s, state_ret[0], DB, L, L)
    o_a_s, nk, nv = _sample_attention(qa_s, ka_s, va_s, cache_win_k[0].reshape(DB, WB, ATT_W),
                                      cache_win_v[0].reshape(DB, WB, ATT_W), DB, L)

    T_all = Tp + Ts
    zero_cnt = jnp.zeros((1, N_EXPERTS), F32)
    x1_p, h2_p, gt_p, p8_p, p8t_p, meta_p, cnt_p = _merge(
        o_r_p, gr, o_a_p, x_prompt.reshape(Tp, D), mp[2], mp[3], mp[4], beta_ret[0], beta_att[0], w_out_bf,
        norm2_g[0], router_w[0], router_b[0], zero_cnt, tmp, S // tmp)
    x1_s, h2_s, gt_s, p8_s, p8t_s, meta_s, cnt = _merge(
        o_r_s, gr_s, o_a_s[None], x_sample.reshape(Ts, D), msm[2], msm[3], msm[4], beta_ret[0], beta_att[0],
        w_out_bf, norm2_g[0], router_w[0], router_b[0], cnt_p, tms, 1)

    tile = min(TOKEN_TILE, tmp)
    assert tile == min(TOKEN_TILE, tms) and tile == Ts, "both groups need the same token tile"
    tme = min(MOE_TILE, T_all)
    meta = jnp.concatenate([meta_p, meta_s], axis=0)
    blk_exp, blk_new, n_used, fill_start, fill_n, chunk_row, tot, nb = _route(
        cnt, meta, T_all * TOP_K, tme, _slots(tile) // SEG)
    fill_max = 1 << (tme + SEG - 2).bit_length() >> 1
    xs = _dispatch(h2_p, h2_s, jnp.concatenate([p8t_p, p8t_s], axis=1), chunk_row, tot, fill_start, fill_n,
                   nb * tme, tile, fill_max)
    yb = _moe_experts(xs, blk_exp, blk_new, n_used, w_up[0], b_up[0], w_down[0], b_down[0], tme)

    tp = Tp // tile
    y_p = _combine(chunk_row[:tp], tot[:tp], yb, p8_p, gt_p, x1_p, mp[5], fp[0], fp[1], final_norm_g,
                   tile, S // tile)
    y_s = _combine(chunk_row[tp:], tot[tp:], yb, p8_s, gt_s, x1_s, msm[5], fs[0], fs[1], final_norm_g,
                   tile, 1)

    shp = (DB, WB, H_ATT, HD_ATT)
    return (y_p.reshape(B, S, D), y_s.reshape(DB, L, D), s_fin[None], s_new[None],
            wk_p[None], wv_p[None], nk.reshape(shp)[None], nv.reshape(shp)[None])
```

```python
import functools
import math

import numpy as np
import jax
import jax.numpy as jnp
from jax import lax
from jax.experimental import pallas as pl
from jax.experimental.pallas import tpu as pltpu

F32 = jnp.float32
BF16 = jnp.bfloat16
HIGHEST = lax.Precision.HIGHEST

D_MODEL = 1024
PAST_LEN = 16384
H_RET, DK_RET, DV_RET = 4, 64, 128
RET_CHUNK = 256
H_ATT, HD_ATT = 8, 64
DILATED_PAIRS = ((128, 1), (512, 4), (2048, 16))
ATT_BLOCK = 128
MAX_WINDOW = 2048
ROPE_THETA = 10000.0
RET_QK = H_RET * DK_RET
RET_V = H_RET * DV_RET
ATT_W = H_ATT * HD_ATT
IN_WIDTH = 2 * RET_QK + 2 * RET_V + 3 * ATT_W
N_EXPERTS = 32
TOP_K = 4
D_FF = 1024
SWIGLU_LIMIT = 7.0
SWIGLU_ALPHA = 1.702
N_MOD = 6
EPS = 1e-6

LANES = 128
VMEM_LIMIT = 56 * 1024 * 1024
ROW_TILE = 512
MOE_TILE = 512
TOKEN_TILE = 256
N_SLABS = ATT_W // LANES
ROW_CHUNKS = D_MODEL // LANES
SEG = 8
ATT_SPAN = 2048

RET_LOG_DECAY = tuple(math.log(1.0 - 2.0 ** (-5.0 - h)) for h in range(H_RET))


def _params(sem, vmem=VMEM_LIMIT):
    return pltpu.CompilerParams(dimension_semantics=sem, vmem_limit_bytes=vmem)


def _sigmoid(z):
    return 1.0 / (1.0 + jnp.exp(-z))


def _adaln_body(c_ref, w_ref, b_ref, o_ref):
    c = c_ref[...]
    s = c * _sigmoid(c)
    o_ref[...] = jnp.dot(s, w_ref[...], precision=HIGHEST, preferred_element_type=F32) + b_ref[...]


def _adaln(c, w, b):
    R, D = c.shape
    N = w.shape[1]
    tn = 1024
    return pl.pallas_call(
        _adaln_body,
        grid=(N // tn,),
        in_specs=[pl.BlockSpec((R, D), lambda j: (0, 0)),
                  pl.BlockSpec((D, tn), lambda j: (0, j)),
                  pl.BlockSpec((1, tn), lambda j: (0, j))],
        out_specs=pl.BlockSpec((R, tn), lambda j: (0, j)),
        out_shape=jax.ShapeDtypeStruct((R, N), F32),
        compiler_params=_params(("arbitrary",)),
        name="adaln",
    )(c, w, b.reshape(1, N))


def _rope_chunk(xc, cos, sin_signed, first_half):
    partner = jnp.where(first_half, pltpu.roll(xc, 96, 1), pltpu.roll(xc, 32, 1))
    return xc * cos + partner * sin_signed


def _inproj_body(x_ref, sh_ref, sc_ref, g_ref, w_ref, cos_ref, sin_ref, *refs, dilated):
    x = x_ref[...]
    tm = x.shape[0]
    ms = jnp.mean(x * x, axis=-1, keepdims=True)
    h = x * lax.rsqrt(ms + EPS) * g_ref[...]
    h = h * (1.0 + sc_ref[...]) + sh_ref[...]
    p = jnp.dot(h.astype(BF16), w_ref[...], preferred_element_type=F32)
    cos = cos_ref[...]
    sin = sin_ref[...]
    first_half = (lax.broadcasted_iota(jnp.int32, (1, LANES), 1) % HD_ATT) < (HD_ATT // 2)

    def chunk(off, c, rope):
        xc = p[:, off + c * LANES: off + (c + 1) * LANES]
        return _rope_chunk(xc, cos, sin, first_half) if rope else xc

    qr_ref, kr_ref, vr_ref, gr_ref = refs[:4]
    o = 0
    for c in range(RET_QK // LANES):
        qr_ref[:, c * LANES:(c + 1) * LANES] = chunk(o, c, True)
    o += RET_QK
    for c in range(RET_QK // LANES):
        kr_ref[:, c * LANES:(c + 1) * LANES] = chunk(o, c, True) * (DK_RET ** -0.5)
    o += RET_QK
    vr_ref[...] = p[:, o:o + RET_V]; o += RET_V
    gr_ref[...] = p[:, o:o + RET_V]; o += RET_V
    if not dilated:
        qa_ref, ka_ref, va_ref = refs[4:7]
        for ref, rope in ((qa_ref, True), (ka_ref, True), (va_ref, False)):
            for c in range(N_SLABS):
                ref[:, c * LANES:(c + 1) * LANES] = chunk(o, c, rope)
            o += ATT_W
        return
    ka_ref, va_ref = refs[4:6]
    slab_refs = refs[6:15]
    scr, scr4 = refs[15:17]
    n4 = tm // 4
    n16 = tm // 16
    for t, (nat_ref, rope, scale) in enumerate(((None, True, HD_ATT ** -0.5), (ka_ref, True, 1.0),
                                                (va_ref, False, 1.0))):
        s1_ref, s4_ref, s16_ref = slab_refs[t], slab_refs[3 + t], slab_refs[6 + t]
        for c in range(N_SLABS):
            xc = chunk(o, c, rope)
            if nat_ref is not None:
                nat_ref[:, c * LANES:(c + 1) * LANES] = xc
            if scale != 1.0:
                xc = xc * scale
            scr[...] = xc
            s1_ref[c] = xc.astype(BF16)
            for r in range(4):
                part = scr[pl.ds(r, n4, stride=4), :]
                s4_ref[c, r] = part.astype(BF16)
                scr4[r] = part
            for r in range(4):
                for r2 in range(4):
                    s16_ref[c, 4 * r2 + r] = scr4[r, pl.ds(r2, n16, stride=4), :].astype(BF16)
        o += ATT_W


def _mod_spec(mod, tm, tiles_per_group):
    R = mod.shape[1]
    return pl.BlockSpec((None, R, mod.shape[2]), lambda i: (i // tiles_per_group, 0, 0))


def _inproj(x2d, sh, sc, g, w_bf, cos_t, sin_t, tm, tiles_per_group, dilated, B=1):
    T, D = x2d.shape
    pos_tiles = cos_t.shape[0] // tm
    row = lambda w: pl.BlockSpec((tm, w), lambda i: (i, 0))
    tab = pl.BlockSpec((tm, LANES), lambda i: (i % pos_tiles, 0))
    f32 = lambda w: jax.ShapeDtypeStruct((T, w), F32)
    if dilated:
        S = T // B
        tpb = S // tm
        widths = (RET_QK, RET_QK, RET_V, RET_V)
        out_specs = [row(w) for w in widths]
        out_shape = [f32(w) for w in widths]
        keep = min(MAX_WINDOW, S)
        assert keep % tm == 0
        wt = keep // tm
        win = pl.BlockSpec((tm, ATT_W),
                           lambda i: ((i // tpb) * wt + jnp.maximum(i % tpb - (tpb - wt), 0), 0))
        out_specs += [win, win]
        out_shape += [jax.ShapeDtypeStruct((B * keep, ATT_W), F32)] * 2
        for dil in (1, 4, 16):
            for _ in range(3):
                if dil == 1:
                    out_specs.append(pl.BlockSpec((None, N_SLABS, tm, LANES),
                                                  lambda i: (i // tpb, 0, i % tpb, 0)))
                    out_shape.append(jax.ShapeDtypeStruct((B, N_SLABS, S, LANES), BF16))
                else:
                    out_specs.append(pl.BlockSpec((None, N_SLABS, dil, tm // dil, LANES),
                                                  lambda i: (i // tpb, 0, 0, i % tpb, 0)))
                    out_shape.append(jax.ShapeDtypeStruct((B, N_SLABS, dil, S // dil, LANES), BF16))
        scratch = [pltpu.VMEM((tm, LANES), F32), pltpu.VMEM((4, tm // 4, LANES), F32)]
    else:
        widths = (RET_QK, RET_QK, RET_V, RET_V, ATT_W, ATT_W, ATT_W)
        out_specs = [row(w) for w in widths]
        out_shape = [f32(w) for w in widths]
        scratch = []
    return pl.pallas_call(
        functools.partial(_inproj_body, dilated=dilated),
        grid=(T // tm,),
        in_specs=[row(D), _mod_spec(sh, tm, tiles_per_group), _mod_spec(sc, tm, tiles_per_group),
                  pl.BlockSpec((1, D), lambda i: (0, 0)),
                  pl.BlockSpec((D, IN_WIDTH), lambda i: (0, 0)), tab, tab],
        out_specs=out_specs, out_shape=out_shape, scratch_shapes=scratch,
        compiler_params=_params(("arbitrary",)),
        name="inproj",
    )(x2d, sh, sc, g.reshape(1, D), w_bf, cos_t, sin_t)


def _rope_tables(pos):
    half = HD_ATT // 2
    inv = ROPE_THETA ** (-jnp.arange(half, dtype=F32) / half)
    ang = pos.astype(F32)[:, None] * inv[None, :]
    cos = jnp.cos(ang)
    sin = jnp.sin(ang)
    cos_t = jnp.tile(cos, (1, LANES // half))
    sin_t = jnp.tile(jnp.concatenate([-sin, sin], axis=1), (1, LANES // HD_ATT))
    return cos_t, sin_t


def _retention_body(q_ref, k_ref, v_ref, s0_ref, o_ref, sfin_ref, s_scr, *, L):
    c = pl.program_id(1)

    @pl.when(c == 0)
    def _():
        s_scr[...] = s0_ref[...]

    ii = lax.broadcasted_iota(jnp.int32, (L, L), 0)
    jj = lax.broadcasted_iota(jnp.int32, (L, L), 1)
    diff = (ii - jj).astype(F32)
    causal = diff >= 0
    row = lax.broadcasted_iota(jnp.int32, (L, 1), 0).astype(F32)
    for h in range(H_RET):
        lg = RET_LOG_DECAY[h]
        decay = jnp.where(causal, jnp.exp(jnp.where(causal, diff, 0.0) * lg), 0.0)
        q = q_ref[:, h * DK_RET:(h + 1) * DK_RET]
        k = k_ref[:, h * DK_RET:(h + 1) * DK_RET]
        v = v_ref[:, h * DV_RET:(h + 1) * DV_RET]
        qb, vb = q.astype(BF16), v.astype(BF16)
        scores = lax.dot_general(qb, k.astype(BF16), (((1,), (1,)), ((), ())),
                                 preferred_element_type=F32) * decay
        S = s_scr[h]
        o = jnp.dot(scores.astype(BF16), vb, preferred_element_type=F32)
        o = o + jnp.dot(qb, S.astype(BF16), preferred_element_type=F32) * jnp.exp((row + 1.0) * lg)
        k_dec = k * jnp.exp((L - 1.0 - row) * lg)
        s_scr[h] = math.exp(L * lg) * S + lax.dot_general(
            k_dec.astype(BF16), vb, (((0,), (0,)), ((), ())), preferred_element_type=F32)
        o_ref[:, h * DV_RET:(h + 1) * DV_RET] = o

    @pl.when(c == pl.num_programs(1) - 1)
    def _():
        sfin_ref[...] = s_scr[...]


def _retention(q2d, k2d, v2d, s0, B, L_seq, L):
    nc = L_seq // L
    row = lambda w: pl.BlockSpec((L, w), lambda b, c: (b * nc + c, 0))
    st = pl.BlockSpec((None, H_RET, DK_RET, DV_RET), lambda b, c: (b, 0, 0, 0))
    return pl.pallas_call(
        functools.partial(_retention_body, L=L),
        grid=(B, nc),
        in_specs=[row(RET_QK), row(RET_QK), row(RET_V), st],
        out_specs=[row(RET_V), st],
        out_shape=[jax.ShapeDtypeStruct((B * L_seq, RET_V), F32),
                   jax.ShapeDtypeStruct((B, H_RET, DK_RET, DV_RET), F32)],
        scratch_shapes=[pltpu.VMEM((H_RET, DK_RET, DV_RET), F32)],
        compiler_params=_params(("arbitrary", "arbitrary")),
        name="retention",
    )(q2d, k2d, v2d, s0)


def _band_attention(q, kcat, vcat, valid, low):
    nt = (((1,), (1,)), ((), ()))
    zero = jnp.zeros_like(q)
    outs = []
    for hh in range(2):
        qh = jnp.where(low if hh == 0 else ~low, q, zero)
        s = lax.dot_general(qh, kcat, nt, preferred_element_type=F32)
        s = jnp.where(valid, s, -jnp.inf)
        m = jnp.max(s, axis=-1, keepdims=True)
        p = jnp.exp(s - m)
        l = jnp.sum(p, axis=-1, keepdims=True)
        outs.append((jnp.dot(p.astype(BF16), vcat, preferred_element_type=F32), m, l))
    (a0, m0, l0), (a1, m1, l1) = outs
    shape = a0.shape
    return (jnp.where(low, a0, a1), jnp.where(low, jnp.broadcast_to(m0, shape), jnp.broadcast_to(m1, shape)),
            jnp.where(low, jnp.broadcast_to(l0, shape), jnp.broadcast_to(l1, shape)))


def _dilated_body(q1, k1, k1p, v1, v1p, q4, k4, k4p, v4, v4p, q16, k16, k16p, v16, v16p,
                  o_ref, kf1, vf1, kf4, vf4, acc_s, m_s, l_s):
    j = pl.program_id(2)
    nq = ATT_BLOCK
    n1 = ATT_SPAN // nq
    GROUP = 4
    kf1[0:nq] = k1p[...]
    kf1[nq:] = k1[...]
    vf1[0:nq] = v1p[...]
    vf1[nq:] = v1[...]
    for r in range(4):
        kf4[r, 0:nq] = k4p[r]
        kf4[r, nq:] = k4[r]
        vf4[r, 0:nq] = v4p[r]
        vf4[r, nq:] = v4[r]
    qi = lax.broadcasted_iota(jnp.int32, (nq, 2 * nq), 0) + nq
    ki = lax.broadcasted_iota(jnp.int32, (nq, 2 * nq), 1)
    dist = qi - ki
    band = (dist >= 0) & (dist <= nq)
    in_cur = ki >= nq
    low = lax.broadcasted_iota(jnp.int32, (1, LANES), 1) < HD_ATT

    def mask(has_prev):
        return band & (has_prev | in_cur)

    def merge(rows, a, m, l):
        m_old, l_old = m_s[rows, :], l_s[rows, :]
        m_new = jnp.maximum(m_old, m)
        w_old, w_new = jnp.exp(m_old - m_new), jnp.exp(m - m_new)
        return acc_s[rows, :] * w_old + a * w_new, m_new, l_old * w_old + l * w_new

    def dil1(g, carry):
        res = []
        for u in range(GROUP):
            qb = g * GROUP + u
            r0 = pl.multiple_of(qb * nq, nq)
            res.append((pl.ds(r0, nq), _band_attention(
                q1[pl.ds(r0, nq), :], kf1[pl.ds(r0, 2 * nq), :], vf1[pl.ds(r0, 2 * nq), :],
                mask((j > 0) | (qb > 0)), low)))
        for rows, (a, m, l) in res:
            acc_s[rows, :] = a
            m_s[rows, :] = m
            l_s[rows, :] = l
        return carry

    lax.fori_loop(0, n1 // GROUP, dil1, 0)

    def dil4(qb, carry):
        r0 = pl.multiple_of(qb * nq, nq)
        res = []
        for r in range(4):
            res.append((pl.ds(qb * (4 * nq) + r, nq, stride=4), _band_attention(
                q4[r, pl.ds(r0, nq), :], kf4[r, pl.ds(r0, 2 * nq), :], vf4[r, pl.ds(r0, 2 * nq), :],
                mask((j > 0) | (qb > 0)), low)))
        for rows, (a, m, l) in res:
            a, m, l = merge(rows, a, m, l)
            acc_s[rows, :] = a
            m_s[rows, :] = m
            l_s[rows, :] = l
        return carry

    lax.fori_loop(0, ATT_SPAN // 4 // nq, dil4, 0)

    def dil16(g, carry):
        res = []
        for u in range(GROUP):
            r = g * GROUP + u
            kcat = jnp.concatenate([k16p[r], k16[r]], axis=0)
            vcat = jnp.concatenate([v16p[r], v16[r]], axis=0)
            res.append((pl.ds(r, nq, stride=16), _band_attention(q16[r], kcat, vcat, mask(j > 0), low)))
        for rows, (a, m, l) in res:
            a, m, l = merge(rows, a, m, l)
            o_ref[rows, :] = a / l
        return carry

    lax.fori_loop(0, 16 // GROUP, dil16, 0)


def _dilated_prompt(slabs, B, S):
    q1, k1, v1, q4, k4, v4, q16, k16, v16 = slabs
    assert S % ATT_SPAN == 0 and DILATED_PAIRS == ((128, 1), (512, 4), (2048, 16))
    nq = ATT_BLOCK
    prev = lambda j, per: jnp.maximum(j * per - 1, 0)
    s1 = pl.BlockSpec((None, None, ATT_SPAN, LANES), lambda b, c, j: (b, c, j, 0))
    s1p = pl.BlockSpec((None, None, nq, LANES), lambda b, c, j: (b, c, prev(j, ATT_SPAN // nq), 0))
    s4 = pl.BlockSpec((None, None, 4, ATT_SPAN // 4, LANES), lambda b, c, j: (b, c, 0, j, 0))
    s4p = pl.BlockSpec((None, None, 4, nq, LANES), lambda b, c, j: (b, c, 0, prev(j, ATT_SPAN // 4 // nq), 0))
    s16 = pl.BlockSpec((None, None, 16, nq, LANES), lambda b, c, j: (b, c, 0, j, 0))
    s16p = pl.BlockSpec((None, None, 16, nq, LANES), lambda b, c, j: (b, c, 0, prev(j, 1), 0))
    return pl.pallas_call(
        _dilated_body,
        grid=(B, N_SLABS, S // ATT_SPAN),
        in_specs=[s1, s1, s1p, s1, s1p, s4, s4, s4p, s4, s4p, s16, s16, s16p, s16, s16p],
        out_specs=pl.BlockSpec((None, None, ATT_SPAN, LANES), lambda b, c, j: (b, c, j, 0)),
        out_shape=jax.ShapeDtypeStruct((B, N_SLABS, S, LANES), F32),
        scratch_shapes=[pltpu.VMEM((ATT_SPAN + nq, LANES), BF16), pltpu.VMEM((ATT_SPAN + nq, LANES), BF16),
                        pltpu.VMEM((4, ATT_SPAN // 4 + nq, LANES), BF16),
                        pltpu.VMEM((4, ATT_SPAN // 4 + nq, LANES), BF16),
                        pltpu.VMEM((ATT_SPAN, LANES), F32), pltpu.VMEM((ATT_SPAN, LANES), F32),
                        pltpu.VMEM((ATT_SPAN, LANES), F32)],
        compiler_params=_params(("arbitrary", "arbitrary", "arbitrary")),
        name="dilated_attention",
    )(q1, k1, k1, v1, v1, q4, k4, k4, v4, v4, q16, k16, k16, v16, v16)


def _sample_multiplicity(L, WB):
    pad = LANES
    cnt = np.zeros((L, WB + pad), np.float32)
    for l in range(L):
        for window, dil in DILATED_PAIRS:
            for j in range(window // dil + 1):
                idx = WB + l - dil * j
                if idx >= 0:
                    cnt[l, idx] += 1.0
    cnt = np.tile(cnt, (H_ATT, 1))
    return cnt[:, :WB], cnt[:, WB:]


def _sample_att_body(q_ref, kn_ref, vn_ref, bk_ref, bv_ref, cb_ref, cn_ref,
                     o_ref, nk_ref, nv_ref, *, L, WB):
    q = q_ref[...]
    lane_head = lax.broadcasted_iota(jnp.int32, (1, ATT_W), 1) // HD_ATT
    qexp = jnp.concatenate([jnp.where(lane_head == h, q, 0.0) for h in range(H_ATT)], axis=0).astype(BF16)
    zpad = jnp.zeros((LANES - L, ATT_W), F32)
    kn = jnp.concatenate([kn_ref[...], zpad], axis=0).astype(BF16)
    vn = jnp.concatenate([vn_ref[...], zpad], axis=0).astype(BF16)
    bk = bk_ref[...]
    bv = bv_ref[...]
    nt = (((1,), (1,)), ((), ()))
    scale = HD_ATT ** -0.5
    sb = lax.dot_general(qexp, bk.astype(BF16), nt, preferred_element_type=F32) * scale
    sn = lax.dot_general(qexp, kn, nt, preferred_element_type=F32) * scale
    cb = cb_ref[...]
    cn = cn_ref[...]
    sb = jnp.where(cb > 0, sb, -jnp.inf)
    sn = jnp.where(cn > 0, sn, -jnp.inf)
    m = jnp.maximum(jnp.max(sb, axis=-1, keepdims=True), jnp.max(sn, axis=-1, keepdims=True))
    pb = cb * jnp.exp(sb - m)
    pn = cn * jnp.exp(sn - m)
    l = jnp.sum(pb, axis=-1, keepdims=True) + jnp.sum(pn, axis=-1, keepdims=True)
    o_all = (jnp.dot(pb.astype(BF16), bv.astype(BF16), preferred_element_type=F32)
             + jnp.dot(pn.astype(BF16), vn, preferred_element_type=F32)) / l
    o = jnp.zeros((L, ATT_W), F32)
    for h in range(H_ATT):
        o = o + jnp.where(lane_head == h, o_all[h * L:(h + 1) * L], 0.0)
    for c in range(N_SLABS):
        o_ref[c] = o[:, c * LANES:(c + 1) * LANES]
    nk_ref[0:WB - L, :] = bk[L:WB]
    nk_ref[WB - L:WB, :] = kn_ref[...]
    nv_ref[0:WB - L, :] = bv[L:WB]
    nv_ref[WB - L:WB, :] = vn_ref[...]


def _sample_attention(q2d, k2d, v2d, buf_k, buf_v, DB, L):
    WB = buf_k.shape[1]
    cb, cn = _sample_multiplicity(L, WB)
    row = pl.BlockSpec((L, ATT_W), lambda b: (b, 0))
    buf = pl.BlockSpec((None, WB, ATT_W), lambda b: (b, 0, 0))
    const = lambda a: pl.BlockSpec(a.shape, lambda b: (0, 0))
    return pl.pallas_call(
        functools.partial(_sample_att_body, L=L, WB=WB),
        grid=(DB,),
        in_specs=[row, row, row, buf, buf, const(cb), const(cn)],
        out_specs=[pl.BlockSpec((N_SLABS, L, LANES), lambda b: (0, b, 0)), buf, buf],
        out_shape=[jax.ShapeDtypeStruct((N_SLABS, DB * L, LANES), F32),
                   jax.ShapeDtypeStruct((DB, WB, ATT_W), F32),
                   jax.ShapeDtypeStruct((DB, WB, ATT_W), F32)],
        compiler_params=_params(("arbitrary",)),
        name="sample_attention",
    )(q2d, k2d, v2d, buf_k, buf_v, jnp.asarray(cb), jnp.asarray(cn))


def _group_mean(sq, width):
    if width == LANES:
        return jnp.mean(sq, axis=-1, keepdims=True)
    lane = lax.broadcasted_iota(jnp.int32, (1, LANES), 1)
    low = lane < width
    s_lo = jnp.sum(jnp.where(low, sq, 0.0), axis=-1, keepdims=True)
    s_hi = jnp.sum(jnp.where(low, 0.0, sq), axis=-1, keepdims=True)
    return jnp.where(low, s_lo, s_hi) * (1.0 / width)


def _merge_body(or_ref, gr_ref, oa0, oa1, oa2, oa3, x_ref, g1_ref, sh_ref, sc_ref, br_ref, ba_ref, wo_ref,
                n2_ref, rw_ref, rb_ref, cin_ref,
                x1_ref, h2_ref, gt_ref, p8_ref, p8t_ref, meta_ref, cnt_ref, cnt_scr):
    i = pl.program_id(0)

    @pl.when(i == 0)
    def _():
        cnt_scr[...] = cin_ref[...]

    parts = []
    for c in range(RET_V // LANES):
        cs = slice(c * LANES, (c + 1) * LANES)
        o = or_ref[:, cs]
        o = o * lax.rsqrt(_group_mean(o * o, DV_RET) + EPS) * br_ref[:, cs]
        g = gr_ref[:, cs]
        parts.append((o * (g * _sigmoid(g))).astype(BF16))
    for c, oa in enumerate((oa0, oa1, oa2, oa3)):
        cs = slice(c * LANES, (c + 1) * LANES)
        o = oa[...]
        o = o * lax.rsqrt(_group_mean(o * o, HD_ATT) + EPS) * ba_ref[:, cs]
        parts.append(o.astype(BF16))
    mix = jnp.concatenate(parts, axis=1)
    y = jnp.dot(mix, wo_ref[...], preferred_element_type=F32)
    x1 = x_ref[...] + g1_ref[...] * y
    x1_ref[...] = x1
    ms = jnp.mean(x1 * x1, axis=-1, keepdims=True)
    h2 = x1 * lax.rsqrt(ms + EPS) * n2_ref[...]
    h2 = h2 * (1.0 + sc_ref[...]) + sh_ref[...]
    rw = rw_ref[...]
    h_hi = h2.astype(BF16)
    h2_ref[...] = h_hi
    h_lo = (h2 - h_hi.astype(F32)).astype(BF16)
    w_hi = rw.astype(BF16)
    w_lo = (rw - w_hi.astype(F32)).astype(BF16)
    logits = (jnp.dot(h_hi, w_hi, preferred_element_type=F32) + jnp.dot(h_lo, w_hi, preferred_element_type=F32)
              + jnp.dot(h_hi, w_lo, preferred_element_type=F32)) + rb_ref[...]
    tm = logits.shape[0]
    lane = lax.broadcasted_iota(jnp.int32, logits.shape, 1).astype(F32)
    vals, idxs = [], []
    for _ in range(TOP_K):
        m = jnp.max(logits, axis=-1, keepdims=True)
        idx = jnp.min(jnp.where(logits == m, lane, float(N_EXPERTS)), axis=-1, keepdims=True)
        vals.append(m)
        idxs.append(idx)
        logits = jnp.where(lane == idx, -jnp.inf, logits)
    e = [jnp.exp(v - vals[0]) for v in vals]
    tot = e[0] + e[1] + e[2] + e[3]
    for k in range(TOP_K):
        gt_ref[:, k:k + 1] = e[k] / tot
    member = jnp.zeros(logits.shape, F32)
    for k in range(TOP_K):
        member = member + jnp.where(lane == idxs[k], 1.0, 0.0)
    sub = min(TOKEN_TILE, tm)
    ti = lax.broadcasted_iota(jnp.int32, (tm, tm), 0)
    tj = lax.broadcasted_iota(jnp.int32, (tm, tm), 1)
    before = jnp.where((tj < ti) & (tj // sub == ti // sub), 1.0, 0.0).astype(BF16)
    rin = jnp.dot(before, member.astype(BF16), preferred_element_type=F32)
    earlier = jnp.where(lax.broadcasted_iota(jnp.int32, (N_EXPERTS, N_EXPERTS), 0)
                        < lax.broadcasted_iota(jnp.int32, (N_EXPERTS, N_EXPERTS), 1), 1.0, 0.0)
    row_sub = lax.broadcasted_iota(jnp.int32, (tm, 1), 0) // sub
    carry = cnt_scr[...]
    off8 = jnp.zeros(logits.shape, F32)
    for s in range(tm // sub):
        in_sub = row_sub == s
        n = jnp.sum(jnp.where(in_sub, member, 0.0), axis=0, keepdims=True)
        n8 = jnp.floor((n + 7.0) * 0.125) * 8.0
        o8 = jnp.dot(jnp.broadcast_to(n8, (8, N_EXPERTS)), earlier, precision=HIGHEST,
                     preferred_element_type=F32)
        o8 = jnp.max(o8, axis=0, keepdims=True)
        off8 = off8 + jnp.where(in_sub, o8, 0.0)
        meta_ref[s, 0:1, :] = n
        meta_ref[s, 1:2, :] = carry
        carry = carry + n
    lane128 = lax.broadcasted_iota(jnp.int32, (tm, LANES), 1)
    slots_k = jnp.zeros((tm, LANES), F32)
    for k in range(TOP_K):
        pos = jnp.sum(jnp.where(lane == idxs[k], rin + off8, 0.0), axis=-1, keepdims=True)
        p8_ref[:, k:k + 1] = pos.astype(jnp.int32)
        slots_k = slots_k + jnp.where(lane128 == k, pos, 0.0)
    p8t_ref[...] = slots_k.T[0:8, :].astype(jnp.int32)
    cnt_scr[...] = carry
    cnt_ref[...] = carry


def _merge(o_r, g_r, o_a, x2d, g1, sh2, sc2, beta_ret, beta_att, wo_bf, n2, rw, rb, cnt_in, tm,
           tiles_per_group):
    T, D = x2d.shape
    tpg = o_a.shape[2] // tm
    sub = min(TOKEN_TILE, tm)
    row = lambda w: pl.BlockSpec((tm, w), lambda i: (i, 0))
    const = lambda r, c: pl.BlockSpec((r, c), lambda i: (0, 0))
    ms = lambda m: _mod_spec(m, tm, tiles_per_group)
    slab = lambda c: pl.BlockSpec((None, None, tm, LANES), lambda i: (i // tpg, c, i % tpg, 0))
    ins = [o_r, g_r, o_a, o_a, o_a, o_a, x2d, g1, sh2, sc2, beta_ret.reshape(1, RET_V),
           beta_att.reshape(1, ATT_W), wo_bf, n2.reshape(1, D), rw, rb.reshape(1, N_EXPERTS), cnt_in]
    in_specs = [row(RET_V), row(RET_V), slab(0), slab(1), slab(2), slab(3), row(D), ms(g1), ms(sh2), ms(sc2),
                const(1, RET_V), const(1, ATT_W), const(D, D), const(1, D),
                const(D, N_EXPERTS), const(1, N_EXPERTS), const(1, N_EXPERTS)]
    return pl.pallas_call(
        _merge_body,
        grid=(T // tm,),
        in_specs=in_specs,
        out_specs=[row(D), row(D), row(TOP_K), row(TOP_K), pl.BlockSpec((8, tm), lambda i: (0, i)),
                   pl.BlockSpec((tm // sub, 2, N_EXPERTS), lambda i: (i, 0, 0)), const(1, N_EXPERTS)],
        out_shape=[jax.ShapeDtypeStruct((T, D), F32), jax.ShapeDtypeStruct((T, D), BF16),
                   jax.ShapeDtypeStruct((T, TOP_K), F32), jax.ShapeDtypeStruct((T, TOP_K), jnp.int32),
                   jax.ShapeDtypeStruct((8, T), jnp.int32),
                   jax.ShapeDtypeStruct((T // sub, 2, N_EXPERTS), F32),
                   jax.ShapeDtypeStruct((1, N_EXPERTS), F32)],
        scratch_shapes=[pltpu.VMEM((1, N_EXPERTS), F32)],
        compiler_params=_params(("arbitrary",)),
        name="merge_router",
    )(*ins)


def _slots(tile):
    return -(-(tile * TOP_K + N_EXPERTS * SEG) // LANES) * LANES


def _route(counts, meta, n_assign, tm, max_chunks):
    nb = -(-(n_assign + N_EXPERTS * (tm - 1 + SEG - 1)) // tm)
    experts = jnp.arange(N_EXPERTS, dtype=jnp.int32)
    counts = counts.reshape(N_EXPERTS).astype(jnp.int32)
    padded = (counts + (SEG - 1) + tm - 1) // tm * tm
    pad_end = jnp.cumsum(padded)
    pad_start = pad_end - padded
    blk0 = jnp.arange(nb + 1, dtype=jnp.int32) * tm
    blk_exp = jnp.minimum(jnp.sum((pad_end[None, :] <= blk0[:, None]).astype(jnp.int32), axis=1),
                          N_EXPERTS - 1).astype(jnp.int32)
    n_used = (pad_end[-1:] // tm).astype(jnp.int32)
    blk_new = jnp.concatenate([jnp.ones((1,), jnp.int32),
                               (blk_exp[1:] != blk_exp[:-1]).astype(jnp.int32)])
    fill_start = jnp.concatenate([pad_start + counts, pad_end[-1:]]).astype(jnp.int32)
    fill_n = jnp.concatenate([padded - counts, nb * tm - pad_end[-1:]]).astype(jnp.int32)
    n = meta[:, 0, :].astype(jnp.int32)
    start = pad_start[None, :] + meta[:, 1, :].astype(jnp.int32)
    nseg = (n + SEG - 1) // SEG
    cum = jnp.cumsum(nseg, axis=1)
    m = jnp.arange(max_chunks, dtype=jnp.int32)
    e_of = jnp.sum((cum[:, None, :] <= m[None, :, None]).astype(jnp.int32), axis=-1)
    onehot = e_of[:, :, None] == experts[None, None, :]
    first = jnp.sum(jnp.where(onehot, (cum - nseg)[:, None, :], 0), axis=-1)
    st = jnp.sum(jnp.where(onehot, start[:, None, :], 0), axis=-1)
    chunk_row = (st + SEG * (m[None, :] - first)).astype(jnp.int32)
    return blk_exp, blk_new, n_used, fill_start, fill_n, chunk_row, cum[:, -1].astype(jnp.int32), nb


def _wait_copies(count, sem, src, dst, rows_per_copy, max_count):
    sz = 1 << (max_count.bit_length() - 1)
    while sz >= 1:
        @pl.when((count & sz) != 0)
        def _(sz=sz):
            pltpu.make_async_copy(src.at[pl.ds(0, sz * rows_per_copy)], dst.at[pl.ds(0, sz * rows_per_copy)],
                                  sem).wait()
        sz //= 2


def _dispatch_body(rows_ref, tot_ref, fs_ref, fn_ref, p8t_ref, hp_ref, hs_ref, xs_hbm,
                   sbuf, zbuf, sem, zsem, *, tm, slots, max_chunks, fill_max, prompt_tiles):
    i = pl.program_id(0)
    slot = i % 2
    seg_rows = SEG * ROW_CHUNKS

    h = jnp.where(i < prompt_tiles, hp_ref[...], hs_ref[...])
    s_iota = lax.broadcasted_iota(jnp.int32, (slots, tm), 0)
    p = jnp.zeros((slots, tm), F32)
    for k in range(TOP_K):
        p = p + jnp.where(s_iota == p8t_ref[k:k + 1, :], 1.0, 0.0)
    srt = jnp.dot(p.astype(BF16), h, preferred_element_type=F32)
    for c in range(ROW_CHUNKS):
        sbuf[slot, pl.ds(c, slots, stride=ROW_CHUNKS), :] = srt[:, c * LANES:(c + 1) * LANES]

    @pl.when(i > 0)
    def _():
        _wait_copies(tot_ref[i - 1], sem.at[1 - slot], sbuf.at[1 - slot], xs_hbm, seg_rows, max_chunks)

    def one(m, carry):
        row = pl.multiple_of(rows_ref[i * max_chunks + m] * ROW_CHUNKS, ROW_CHUNKS)
        src = pl.multiple_of(m * seg_rows, seg_rows)
        pltpu.make_async_copy(sbuf.at[slot, pl.ds(src, seg_rows)], xs_hbm.at[pl.ds(row, seg_rows)],
                              sem.at[slot]).start()
        return carry

    lax.fori_loop(0, tot_ref[i], one, 0)

    @pl.when(i == pl.num_programs(0) - 1)
    def _():
        _wait_copies(tot_ref[i], sem.at[slot], sbuf.at[slot], xs_hbm, seg_rows, max_chunks)
        zbuf[...] = jnp.zeros(zbuf.shape, F32)

        def fill(e, wait):
            start = fs_ref[e]
            n = fn_ref[e]
            sz = fill_max
            while sz >= 1:
                @pl.when((n & sz) != 0)
                def _(start=start, sz=sz):
                    cp = pltpu.make_async_copy(
                        zbuf.at[pl.ds(0, sz * ROW_CHUNKS)],
                        xs_hbm.at[pl.ds(pl.multiple_of(start * ROW_CHUNKS, ROW_CHUNKS), sz * ROW_CHUNKS)], zsem)
                    cp.wait() if wait else cp.start()
                start = start + (n & sz)
                sz //= 2

        def fill_all(wait):
            def body(e, carry):
                fill(e, wait)
                return carry
            lax.fori_loop(0, N_EXPERTS, body, 0)

            def tail(c, carry):
                row = pl.multiple_of((fs_ref[N_EXPERTS] + c * fill_max) * ROW_CHUNKS, ROW_CHUNKS)
                cp = pltpu.make_async_copy(zbuf, xs_hbm.at[pl.ds(row, fill_max * ROW_CHUNKS)], zsem)
                cp.wait() if wait else cp.start()
                return carry
            lax.fori_loop(0, fn_ref[N_EXPERTS] // fill_max, tail, 0)

        fill_all(False)
        fill_all(True)


def _dispatch(h2_p, h2_s, p8t, chunk_row, tot, fill_start, fill_n, R, tm, fill_max):
    Tp, D = h2_p.shape
    Ts = h2_s.shape[0]
    np_, ns_ = Tp // tm, Ts // tm
    slots = _slots(tm)
    max_chunks = slots // SEG
    grid_spec = pltpu.PrefetchScalarGridSpec(
        num_scalar_prefetch=4,
        grid=(np_ + ns_,),
        in_specs=[pl.BlockSpec((8, tm), lambda i, *_: (0, i)),
                  pl.BlockSpec((tm, D), lambda i, *_: (jnp.minimum(i, np_ - 1), 0)),
                  pl.BlockSpec((tm, D), lambda i, *_: (jnp.maximum(i - np_, 0), 0))],
        out_specs=pl.BlockSpec(memory_space=pl.ANY),
        scratch_shapes=[pltpu.VMEM((2, slots * ROW_CHUNKS, LANES), F32),
                        pltpu.VMEM((fill_max * ROW_CHUNKS, LANES), F32),
                        pltpu.SemaphoreType.DMA((2,)), pltpu.SemaphoreType.DMA(())],
    )
    return pl.pallas_call(
        functools.partial(_dispatch_body, tm=tm, slots=slots, max_chunks=max_chunks, fill_max=fill_max,
                          prompt_tiles=np_),
        grid_spec=grid_spec,
        out_shape=jax.ShapeDtypeStruct((R * ROW_CHUNKS, LANES), F32),
        compiler_params=_params(("arbitrary",)),
        name="moe_dispatch",
    )(chunk_row.reshape(-1), tot, fill_start, fill_n, p8t, h2_p, h2_s)


def _moe_body(nused_ref, bexp_ref, new_ref, x_ref, wu_ref, bu_ref, wd_ref, bd_ref, y_ref, wu_bf, wd_bf):
    i = pl.program_id(0)

    @pl.when(new_ref[i] != 0)
    def _():
        wu_bf[...] = wu_ref[...].astype(BF16)
        wd_bf[...] = wd_ref[...].astype(BF16)

    @pl.when(i < nused_ref[0])
    def _():
        tm = x_ref.shape[0] // ROW_CHUNKS
        x = jnp.concatenate([x_ref[pl.ds(c, tm, stride=ROW_CHUNKS), :].astype(BF16)
                             for c in range(ROW_CHUNKS)], axis=1)
        u = jnp.dot(x, wu_bf[...], preferred_element_type=F32) + bu_ref[...]
        x_glu = jnp.minimum(u[:, :D_FF], SWIGLU_LIMIT)
        x_lin = jnp.clip(u[:, D_FF:], -SWIGLU_LIMIT, SWIGLU_LIMIT)
        act = x_glu * _sigmoid(SWIGLU_ALPHA * x_glu) * (x_lin + 1.0)
        y = jnp.dot(act.astype(BF16), wd_bf[...], preferred_element_type=F32) + bd_ref[...]
        for c in range(ROW_CHUNKS):
            y_ref[pl.ds(c, tm, stride=ROW_CHUNKS), :] = y[:, c * LANES:(c + 1) * LANES]

    @pl.when(i >= nused_ref[0])
    def _():
        y_ref[...] = jnp.zeros(y_ref.shape, F32)


def _moe_experts(xs, blk_exp, blk_new, n_used, w_up, bu, w_down, bd, tm):
    R, D = xs.shape[0] // ROW_CHUNKS, D_MODEL
    blk = lambda i, nu, be, nw: (jnp.minimum(i, nu[0] - 1), 0)
    wspec = lambda r, c: pl.BlockSpec((None, r, c), lambda i, nu, be, nw: (be[i], 0, 0))
    grid_spec = pltpu.PrefetchScalarGridSpec(
        num_scalar_prefetch=3,
        grid=(R // tm + 1,),
        in_specs=[pl.BlockSpec((tm * ROW_CHUNKS, LANES), blk), wspec(D, 2 * D_FF), wspec(1, 2 * D_FF),
                  wspec(D_FF, D), wspec(1, D)],
        out_specs=pl.BlockSpec((tm * ROW_CHUNKS, LANES), lambda i, nu, be, nw: (i, 0)),
        scratch_shapes=[pltpu.VMEM((D, 2 * D_FF), BF16), pltpu.VMEM((D_FF, D), BF16)],
    )
    return pl.pallas_call(
        _moe_body,
        grid_spec=grid_spec,
        out_shape=jax.ShapeDtypeStruct(((R + tm) * ROW_CHUNKS, LANES), F32),
        compiler_params=_params(("arbitrary",)),
        name="moe_experts",
    )(n_used, blk_exp, blk_new, xs, w_up, bu.reshape(N_EXPERTS, 1, 2 * D_FF), w_down,
      bd.reshape(N_EXPERTS, 1, D))


def _combine_body(rows_ref, tot_ref, yb_hbm, p8_ref, gate_ref, x1_ref, g2_ref, shf_ref, scf_ref, gf_ref,
                  out_ref, sbuf, sem, *, tm, slots, max_chunks):
    i = pl.program_id(0)
    nsteps = pl.num_programs(0)
    seg_rows = SEG * ROW_CHUNKS

    @pl.when(i == 0)
    def _():
        sbuf[...] = jnp.zeros(sbuf.shape, F32)

    def start_runs(tile, slot):
        def one(m, carry):
            src = pl.multiple_of(rows_ref[tile * max_chunks + m] * ROW_CHUNKS, ROW_CHUNKS)
            dst = pl.multiple_of(m * seg_rows, seg_rows)
            pltpu.make_async_copy(yb_hbm.at[pl.ds(src, seg_rows)], sbuf.at[slot, pl.ds(dst, seg_rows)],
                                  sem.at[slot]).start()
            return carry
        lax.fori_loop(0, tot_ref[tile], one, 0)

    @pl.when(i == 0)
    def _():
        start_runs(0, 0)

    @pl.when(i + 1 < nsteps)
    def _():
        start_runs(i + 1, (i + 1) % 2)

    slot = i % 2
    _wait_copies(tot_ref[i], sem.at[slot], yb_hbm, sbuf.at[slot], seg_rows, max_chunks)
    pos = p8_ref[...]
    gates = gate_ref[...]
    s_iota = lax.broadcasted_iota(jnp.int32, (tm, slots), 1)
    g = jnp.zeros((tm, slots), F32)
    for k in range(TOP_K):
        g = g + jnp.where(s_iota == pos[:, k:k + 1], gates[:, k:k + 1], 0.0)
    gb = g.astype(BF16)
    ys = []
    for c in range(ROW_CHUNKS):
        chunk = sbuf[slot, pl.ds(c, slots, stride=ROW_CHUNKS), :]
        ys.append(jnp.dot(gb, chunk.astype(BF16), preferred_element_type=F32))
    y = jnp.concatenate(ys, axis=1)
    x2 = x1_ref[...] + g2_ref[...] * y
    ms = jnp.mean(x2 * x2, axis=-1, keepdims=True)
    h = x2 * lax.rsqrt(ms + EPS) * gf_ref[...]
    out_ref[...] = h * (1.0 + scf_ref[...]) + shf_ref[...]


def _combine(chunk_row, tot, yb, pos8, gates, x1, g2, shf, scf, gf, tm, tiles_per_group):
    T, D = x1.shape
    nsteps = T // tm
    slots = _slots(tm)
    max_chunks = slots // SEG
    row = lambda w: pl.BlockSpec((tm, w), lambda i, rows, tot: (i, 0))
    ms = lambda m: pl.BlockSpec((None, m.shape[1], m.shape[2]),
                                lambda i, rows, tot: (i // tiles_per_group, 0, 0))
    grid_spec = pltpu.PrefetchScalarGridSpec(
        num_scalar_prefetch=2,
        grid=(nsteps,),
        in_specs=[pl.BlockSpec(memory_space=pl.ANY), row(TOP_K), row(TOP_K), row(D), ms(g2), ms(shf), ms(scf),
                  pl.BlockSpec((1, D), lambda i, rows, tot: (0, 0))],
        out_specs=row(D),
        scratch_shapes=[pltpu.VMEM((2, slots * ROW_CHUNKS, LANES), F32), pltpu.SemaphoreType.DMA((2,))],
    )
    return pl.pallas_call(
        functools.partial(_combine_body, tm=tm, slots=slots, max_chunks=max_chunks),
        grid_spec=grid_spec,
        out_shape=jax.ShapeDtypeStruct((T, D), F32),
        compiler_params=_params(("arbitrary",)),
        name="combine_final",
    )(chunk_row.reshape(-1), tot, yb, pos8, gates, x1, g2, shf, scf, gf.reshape(1, D))


def _group_mods(mod, n, rows_per_group, per_row):
    outs = []
    for j in range(n):
        m = mod[:, j * D_MODEL:(j + 1) * D_MODEL]
        if per_row:
            outs.append(jnp.repeat(m, rows_per_group, axis=0)[None])
        else:
            outs.append(m[:, None, :])
    return outs


def kernel(x_prompt, x_sample, state_ret, cache_win_k, cache_win_v, c_prompt, c_sample,
           ada_w, ada_b, norm1_g, w_in, beta_ret, beta_att, w_out, norm2_g,
           router_w, router_b, w_up, b_up, w_down, b_down,
           final_ada_w, final_ada_b, final_norm_g):
    B, S, D = x_prompt.shape
    DB, L = x_sample.shape[:2]
    WB = cache_win_k.shape[2]
    Tp, Ts = B * S, DB * L
    assert ada_w.shape[0] == 1, "single-layer trunk"
    c_all = jnp.concatenate([c_prompt, c_sample], axis=0)
    c_all = jnp.pad(c_all, ((0, (-c_all.shape[0]) % 8), (0, 0)))
    mods = _adaln(c_all, ada_w[0], ada_b[0])
    fmods = _adaln(c_all, final_ada_w, final_ada_b)
    w_in_bf, w_out_bf = w_in[0].astype(BF16), w_out[0].astype(BF16)

    tmp = min(ROW_TILE, S)
    tms = Ts
    assert Tp % tmp == 0
    mp = _group_mods(mods[:B], N_MOD, S, False)
    msm = _group_mods(mods[B:B + DB], N_MOD, L, True)
    fp = _group_mods(fmods[:B], 2, S, False)
    fs = _group_mods(fmods[B:B + DB], 2, L, True)

    cos_p, sin_p = _rope_tables(jnp.arange(S))
    outs = _inproj(x_prompt.reshape(Tp, D), mp[0], mp[1], norm1_g[0], w_in_bf, cos_p, sin_p,
                   tmp, S // tmp, True, B)
    qr, kr, vr, gr, ka, va = outs[:6]
    o_r_p, s_fin = _retention(qr, kr, vr, jnp.zeros((B, H_RET, DK_RET, DV_RET), F32), B, S, RET_CHUNK)
    o_a_p = _dilated_prompt(outs[6:], B, S)
    keep = min(MAX_WINDOW, S)
    win = lambda t: t.reshape(B, keep, H_ATT, HD_ATT)
    wk_p, wv_p = win(ka), win(va)

    cos_s, sin_s = _rope_tables(PAST_LEN + jnp.arange(L))
    cos_s, sin_s = jnp.tile(cos_s, (DB, 1)), jnp.tile(sin_s, (DB, 1))
    qr_s, kr_s, vr_s, gr_s, qa_s, ka_s, va_s = _inproj(
        x_sample.reshape(Ts, D), msm[0], msm[1], norm1_g[0], w_in_bf, cos_s, sin_s, tms, 1, False)
    o_r_s, s_new = _retention(qr_s, kr_s, vr_s, state_ret[0], DB, L, L)
    o_a_s, nk, nv = _sample_attention(qa_s, ka_s, va_s, cache_win_k[0].reshape(DB, WB, ATT_W),
                                      cache_win_v[0].reshape(DB, WB, ATT_W), DB, L)

    T_all = Tp + Ts
    zero_cnt = jnp.zeros((1, N_EXPERTS), F32)
    x1_p, h2_p, gt_p, p8_p, p8t_p, meta_p, cnt_p = _merge(
        o_r_p, gr, o_a_p, x_prompt.reshape(Tp, D), mp[2], mp[3], mp[4], beta_ret[0], beta_att[0], w_out_bf,
        norm2_g[0], router_w[0], router_b[0], zero_cnt, tmp, S // tmp)
    x1_s, h2_s, gt_s, p8_s, p8t_s, meta_s, cnt = _merge(
        o_r_s, gr_s, o_a_s[None], x_sample.reshape(Ts, D), msm[2], msm[3], msm[4], beta_ret[0], beta_att[0],
        w_out_bf, norm2_g[0], router_w[0], router_b[0], cnt_p, tms, 1)

    tile = min(TOKEN_TILE, tmp)
    assert tile == min(TOKEN_TILE, tms) and tile == Ts, "both groups need the same token tile"
    tme = min(MOE_TILE, T_all)
    meta = jnp.concatenate([meta_p, meta_s], axis=0)
    blk_exp, blk_new, n_used, fill_start, fill_n, chunk_row, tot, nb = _route(
        cnt, meta, T_all * TOP_K, tme, _slots(tile) // SEG)
    fill_max = 1 << (tme + SEG - 2).bit_length() >> 1
    xs = _dispatch(h2_p, h2_s, jnp.concatenate([p8t_p, p8t_s], axis=1), chunk_row, tot, fill_start, fill_n,
                   nb * tme, tile, fill_max)
    yb = _moe_experts(xs, blk_exp, blk_new, n_used, w_up[0], b_up[0], w_down[0], b_down[0], tme)

    tp = Tp // tile
    y_p = _combine(chunk_row[:tp], tot[:tp], yb, p8_p, gt_p, x1_p, mp[5], fp[0], fp[1], final_norm_g,
                   tile, S // tile)
    y_s = _combine(chunk_row[tp:], tot[tp:], yb, p8_s, gt_s, x1_s, msm[5], fs[0], fs[1], final_norm_g,
                   tile, 1)

    shp = (DB, WB, H_ATT, HD_ATT)
    return (y_p.reshape(B, S, D), y_s.reshape(DB, L, D), s_fin[None], s_new[None],
            wk_p[None], wv_p[None], nk.reshape(shp)[None], nv.reshape(shp)[None])
```

```python
import functools
import math

import numpy as np
import jax
import jax.numpy as jnp
from jax import lax
from jax.experimental import pallas as pl
from jax.experimental.pallas import tpu as pltpu

F32 = jnp.float32
BF16 = jnp.bfloat16
HIGHEST = lax.Precision.HIGHEST

D_MODEL = 1024
PAST_LEN = 16384
H_RET, DK_RET, DV_RET = 4, 64, 128
RET_CHUNK = 256
H_ATT, HD_ATT = 8, 64
DILATED_PAIRS = ((128, 1), (512, 4), (2048, 16))
ATT_BLOCK = 128
MAX_WINDOW = 2048
ROPE_THETA = 10000.0
RET_QK = H_RET * DK_RET
RET_V = H_RET * DV_RET
ATT_W = H_ATT * HD_ATT
IN_WIDTH = 2 * RET_QK + 2 * RET_V + 3 * ATT_W
N_EXPERTS = 32
TOP_K = 4
D_FF = 1024
SWIGLU_LIMIT = 7.0
SWIGLU_ALPHA = 1.702
N_MOD = 6
EPS = 1e-6

LANES = 128
VMEM_LIMIT = 56 * 1024 * 1024
ROW_TILE = 512
MOE_TILE = 512
TOKEN_TILE = 256
N_SLABS = ATT_W // LANES
ROW_CHUNKS = D_MODEL // LANES
SEG = 8
ATT_SPAN = 2048

RET_LOG_DECAY = tuple(math.log(1.0 - 2.0 ** (-5.0 - h)) for h in range(H_RET))


def _params(sem, vmem=VMEM_LIMIT):
    return pltpu.CompilerParams(dimension_semantics=sem, vmem_limit_bytes=vmem)


def _sigmoid(z):
    return 1.0 / (1.0 + jnp.exp(-z))


def _adaln_body(c_ref, w_ref, b_ref, o_ref):
    c = c_ref[...]
    s = c * _sigmoid(c)
    o_ref[...] = jnp.dot(s, w_ref[...], precision=HIGHEST, preferred_element_type=F32) + b_ref[...]


def _adaln(c, w, b):
    R, D = c.shape
    N = w.shape[1]
    tn = 1024
    return pl.pallas_call(
        _adaln_body,
        grid=(N // tn,),
        in_specs=[pl.BlockSpec((R, D), lambda j: (0, 0)),
                  pl.BlockSpec((D, tn), lambda j: (0, j)),
                  pl.BlockSpec((1, tn), lambda j: (0, j))],
        out_specs=pl.BlockSpec((R, tn), lambda j: (0, j)),
        out_shape=jax.ShapeDtypeStruct((R, N), F32),
        compiler_params=_params(("arbitrary",)),
        name="adaln",
    )(c, w, b.reshape(1, N))


def _rope_chunk(xc, cos, sin_signed, first_half):
    partner = jnp.where(first_half, pltpu.roll(xc, 96, 1), pltpu.roll(xc, 32, 1))
    return xc * cos + partner * sin_signed


def _inproj_body(x_ref, sh_ref, sc_ref, g_ref, w_ref, cos_ref, sin_ref, *refs, dilated):
    x = x_ref[...]
    tm = x.shape[0]
    ms = jnp.mean(x * x, axis=-1, keepdims=True)
    h = x * lax.rsqrt(ms + EPS) * g_ref[...]
    h = h * (1.0 + sc_ref[...]) + sh_ref[...]
    p = jnp.dot(h.astype(BF16), w_ref[...], preferred_element_type=F32)
    cos = cos_ref[...]
    sin = sin_ref[...]
    first_half = (lax.broadcasted_iota(jnp.int32, (1, LANES), 1) % HD_ATT) < (HD_ATT // 2)

    def chunk(off, c, rope):
        xc = p[:, off + c * LANES: off + (c + 1) * LANES]
        return _rope_chunk(xc, cos, sin, first_half) if rope else xc

    qr_ref, kr_ref, vr_ref, gr_ref = refs[:4]
    o = 0
    for c in range(RET_QK // LANES):
        qr_ref[:, c * LANES:(c + 1) * LANES] = chunk(o, c, True)
    o += RET_QK
    for c in range(RET_QK // LANES):
        kr_ref[:, c * LANES:(c + 1) * LANES] = chunk(o, c, True) * (DK_RET ** -0.5)
    o += RET_QK
    vr_ref[...] = p[:, o:o + RET_V]; o += RET_V
    gr_ref[...] = p[:, o:o + RET_V]; o += RET_V
    if not dilated:
        qa_ref, ka_ref, va_ref = refs[4:7]
        for ref, rope in ((qa_ref, True), (ka_ref, True), (va_ref, False)):
            for c in range(N_SLABS):
                ref[:, c * LANES:(c + 1) * LANES] = chunk(o, c, rope)
            o += ATT_W
        return
    ka_ref, va_ref = refs[4:6]
    slab_refs = refs[6:15]
    scr, scr4 = refs[15:17]
    n4 = tm // 4
    n16 = tm // 16
    for t, (nat_ref, rope, scale) in enumerate(((None, True, HD_ATT ** -0.5), (ka_ref, True, 1.0),
                                                (va_ref, False, 1.0))):
        s1_ref, s4_ref, s16_ref = slab_refs[t], slab_refs[3 + t], slab_refs[6 + t]
        for c in range(N_SLABS):
            xc = chunk(o, c, rope)
            if nat_ref is not None:
                nat_ref[:, c * LANES:(c + 1) * LANES] = xc
            if scale != 1.0:
                xc = xc * scale
            scr[...] = xc
            s1_ref[c] = xc.astype(BF16)
            for r in range(4):
                part = scr[pl.ds(r, n4, stride=4), :]
                s4_ref[c, r] = part.astype(BF16)
                scr4[r] = part
            for r in range(4):
                for r2 in range(4):
                    s16_ref[c, 4 * r2 + r] = scr4[r, pl.ds(r2, n16, stride=4), :].astype(BF16)
        o += ATT_W


def _mod_spec(mod, tm, tiles_per_group):
    R = mod.shape[1]
    return pl.BlockSpec((None, R, mod.shape[2]), lambda i: (i // tiles_per_group, 0, 0))


def _inproj(x2d, sh, sc, g, w_bf, cos_t, sin_t, tm, tiles_per_group, dilated, B=1):
    T, D = x2d.shape
    pos_tiles = cos_t.shape[0] // tm
    row = lambda w: pl.BlockSpec((tm, w), lambda i: (i, 0))
    tab = pl.BlockSpec((tm, LANES), lambda i: (i % pos_tiles, 0))
    f32 = lambda w: jax.ShapeDtypeStruct((T, w), F32)
    if dilated:
        S = T // B
        tpb = S // tm
        widths = (RET_QK, RET_QK, RET_V, RET_V)
        out_specs = [row(w) for w in widths]
        out_shape = [f32(w) for w in widths]
        keep = min(MAX_WINDOW, S)
        assert keep % tm == 0
        wt = keep // tm
        win = pl.BlockSpec((tm, ATT_W),
                           lambda i: ((i // tpb) * wt + jnp.maximum(i % tpb - (tpb - wt), 0), 0))
        out_specs += [win, win]
        out_shape += [jax.ShapeDtypeStruct((B * keep, ATT_W), F32)] * 2
        for dil in (1, 4, 16):
            for _ in range(3):
                if dil == 1:
                    out_specs.append(pl.BlockSpec((None, N_SLABS, tm, LANES),
                                                  lambda i: (i // tpb, 0, i % tpb, 0)))
                    out_shape.append(jax.ShapeDtypeStruct((B, N_SLABS, S, LANES), BF16))
                else:
                    out_specs.append(pl.BlockSpec((None, N_SLABS, dil, tm // dil, LANES),
                                                  lambda i: (i // tpb, 0, 0, i % tpb, 0)))
                    out_shape.append(jax.ShapeDtypeStruct((B, N_SLABS, dil, S // dil, LANES), BF16))
        scratch = [pltpu.VMEM((tm, LANES), F32), pltpu.VMEM((4, tm // 4, LANES), F32)]
    else:
        widths = (RET_QK, RET_QK, RET_V, RET_V, ATT_W, ATT_W, ATT_W)
        out_specs = [row(w) for w in widths]
        out_shape = [f32(w) for w in widths]
        scratch = []
    return pl.pallas_call(
        functools.partial(_inproj_body, dilated=dilated),
        grid=(T // tm,),
        in_specs=[row(D), _mod_spec(sh, tm, tiles_per_group), _mod_spec(sc, tm, tiles_per_group),
                  pl.BlockSpec((1, D), lambda i: (0, 0)),
                  pl.BlockSpec((D, IN_WIDTH), lambda i: (0, 0)), tab, tab],
        out_specs=out_specs, out_shape=out_shape, scratch_shapes=scratch,
        compiler_params=_params(("arbitrary",)),
        name="inproj",
    )(x2d, sh, sc, g.reshape(1, D), w_bf, cos_t, sin_t)


def _rope_tables(pos):
    half = HD_ATT // 2
    inv = ROPE_THETA ** (-jnp.arange(half, dtype=F32) / half)
    ang = pos.astype(F32)[:, None] * inv[None, :]
    cos = jnp.cos(ang)
    sin = jnp.sin(ang)
    cos_t = jnp.tile(cos, (1, LANES // half))
    sin_t = jnp.tile(jnp.concatenate([-sin, sin], axis=1), (1, LANES // HD_ATT))
    return cos_t, sin_t


def _retention_body(q_ref, k_ref, v_ref, s0_ref, o_ref, sfin_ref, s_scr, *, L):
    c = pl.program_id(1)

    @pl.when(c == 0)
    def _():
        s_scr[...] = s0_ref[...]

    ii = lax.broadcasted_iota(jnp.int32, (L, L), 0)
    jj = lax.broadcasted_iota(jnp.int32, (L, L), 1)
    diff = (ii - jj).astype(F32)
    causal = diff >= 0
    row = lax.broadcasted_iota(jnp.int32, (L, 1), 0).astype(F32)
    for h in range(H_RET):
        lg = RET_LOG_DECAY[h]
        decay = jnp.where(causal, jnp.exp(jnp.where(causal, diff, 0.0) * lg), 0.0)
        q = q_ref[:, h * DK_RET:(h + 1) * DK_RET]
        k = k_ref[:, h * DK_RET:(h + 1) * DK_RET]
        v = v_ref[:, h * DV_RET:(h + 1) * DV_RET]
        qb, vb = q.astype(BF16), v.astype(BF16)
        scores = lax.dot_general(qb, k.astype(BF16), (((1,), (1,)), ((), ())),
                                 preferred_element_type=F32) * decay
        S = s_scr[h]
        o = jnp.dot(scores.astype(BF16), vb, preferred_element_type=F32)
        o = o + jnp.dot(qb, S.astype(BF16), preferred_element_type=F32) * jnp.exp((row + 1.0) * lg)
        k_dec = k * jnp.exp((L - 1.0 - row) * lg)
        s_scr[h] = math.exp(L * lg) * S + lax.dot_general(
            k_dec.astype(BF16), vb, (((0,), (0,)), ((), ())), preferred_element_type=F32)
        o_ref[:, h * DV_RET:(h + 1) * DV_RET] = o

    @pl.when(c == pl.num_programs(1) - 1)
    def _():
        sfin_ref[...] = s_scr[...]


def _retention(q2d, k2d, v2d, s0, B, L_seq, L):
    nc = L_seq // L
    row = lambda w: pl.BlockSpec((L, w), lambda b, c: (b * nc + c, 0))
    st = pl.BlockSpec((None, H_RET, DK_RET, DV_RET), lambda b, c: (b, 0, 0, 0))
    return pl.pallas_call(
        functools.partial(_retention_body, L=L),
        grid=(B, nc),
        in_specs=[row(RET_QK), row(RET_QK), row(RET_V), st],
        out_specs=[row(RET_V), st],
        out_shape=[jax.ShapeDtypeStruct((B * L_seq, RET_V), F32),
                   jax.ShapeDtypeStruct((B, H_RET, DK_RET, DV_RET), F32)],
        scratch_shapes=[pltpu.VMEM((H_RET, DK_RET, DV_RET), F32)],
        compiler_params=_params(("arbitrary", "arbitrary")),
        name="retention",
    )(q2d, k2d, v2d, s0)


def _band_attention(q, kcat, vcat, valid, low):
    nt = (((1,), (1,)), ((), ()))
    zero = jnp.zeros_like(q)
    outs = []
    for hh in range(2):
        qh = jnp.where(low if hh == 0 else ~low, q, zero)
        s = lax.dot_general(qh, kcat, nt, preferred_element_type=F32)
        s = jnp.where(valid, s, -jnp.inf)
        m = jnp.max(s, axis=-1, keepdims=True)
        p = jnp.exp(s - m)
        l = jnp.sum(p, axis=-1, keepdims=True)
        outs.append((jnp.dot(p.astype(BF16), vcat, preferred_element_type=F32), m, l))
    (a0, m0, l0), (a1, m1, l1) = outs
    shape = a0.shape
    return (jnp.where(low, a0, a1), jnp.where(low, jnp.broadcast_to(m0, shape), jnp.broadcast_to(m1, shape)),
            jnp.where(low, jnp.broadcast_to(l0, shape), jnp.broadcast_to(l1, shape)))


def _dilated_body(q1, k1, k1p, v1, v1p, q4, k4, k4p, v4, v4p, q16, k16, k16p, v16, v16p,
                  o_ref, kf1, vf1, kf4, vf4, acc_s, m_s, l_s):
    j = pl.program_id(2)
    nq = ATT_BLOCK
    n1 = ATT_SPAN // nq
    GROUP = 4
    kf1[0:nq] = k1p[...]
    kf1[nq:] = k1[...]
    vf1[0:nq] = v1p[...]
    vf1[nq:] = v1[...]
    for r in range(4):
        kf4[r, 0:nq] = k4p[r]
        kf4[r, nq:] = k4[r]
        vf4[r, 0:nq] = v4p[r]
        vf4[r, nq:] = v4[r]
    qi = lax.broadcasted_iota(jnp.int32, (nq, 2 * nq), 0) + nq
    ki = lax.broadcasted_iota(jnp.int32, (nq, 2 * nq), 1)
    dist = qi - ki
    band = (dist >= 0) & (dist <= nq)
    in_cur = ki >= nq
    low = lax.broadcasted_iota(jnp.int32, (1, LANES), 1) < HD_ATT

    def mask(has_prev):
        return band & (has_prev | in_cur)

    def merge(rows, a, m, l):
        m_old, l_old = m_s[rows, :], l_s[rows, :]
        m_new = jnp.maximum(m_old, m)
        w_old, w_new = jnp.exp(m_old - m_new), jnp.exp(m - m_new)
        return acc_s[rows, :] * w_old + a * w_new, m_new, l_old * w_old + l * w_new

    def dil1(g, carry):
        res = []
        for u in range(GROUP):
            qb = g * GROUP + u
            r0 = pl.multiple_of(qb * nq, nq)
            res.append((pl.ds(r0, nq), _band_attention(
                q1[pl.ds(r0, nq), :], kf1[pl.ds(r0, 2 * nq), :], vf1[pl.ds(r0, 2 * nq), :],
                mask((j > 0) | (qb > 0)), low)))
        for rows, (a, m, l) in res:
            acc_s[rows, :] = a
            m_s[rows, :] = m
            l_s[rows, :] = l
        return carry

    lax.fori_loop(0, n1 // GROUP, dil1, 0)

    def dil4(qb, carry):
        r0 = pl.multiple_of(qb * nq, nq)
        res = []
        for r in range(4):
            res.append((pl.ds(qb * (4 * nq) + r, nq, stride=4), _band_attention(
                q4[r, pl.ds(r0, nq), :], kf4[r, pl.ds(r0, 2 * nq), :], vf4[r, pl.ds(r0, 2 * nq), :],
                mask((j > 0) | (qb > 0)), low)))
        for rows, (a, m, l) in res:
            a, m, l = merge(rows, a, m, l)
            acc_s[rows, :] = a
            m_s[rows, :] = m
            l_s[rows, :] = l
        return carry

    lax.fori_loop(0, ATT_SPAN // 4 // nq, dil4, 0)

    def dil16(g, carry):
        res = []
        for u in range(GROUP):
            r = g * GROUP + u
            kcat = jnp.concatenate([k16p[r], k16[r]], axis=0)
            vcat = jnp.concatenate([v16p[r], v16[r]], axis=0)
            res.append((pl.ds(r, nq, stride=16), _band_attention(q16[r], kcat, vcat, mask(j > 0), low)))
        for rows, (a, m, l) in res:
            a, m, l = merge(rows, a, m, l)
            o_ref[rows, :] = a / l
        return carry

    lax.fori_loop(0, 16 // GROUP, dil16, 0)


def _dilated_prompt(slabs, B, S):
    q1, k1, v1, q4, k4, v4, q16, k16, v16 = slabs
    assert S % ATT_SPAN == 0 and DILATED_PAIRS == ((128, 1), (512, 4), (2048, 16))
    nq = ATT_BLOCK
    prev = lambda j, per: jnp.maximum(j * per - 1, 0)
    s1 = pl.BlockSpec((None, None, ATT_SPAN, LANES), lambda b, c, j: (b, c, j, 0))
    s1p = pl.BlockSpec((None, None, nq, LANES), lambda b, c, j: (b, c, prev(j, ATT_SPAN // nq), 0))
    s4 = pl.BlockSpec((None, None, 4, ATT_SPAN // 4, LANES), lambda b, c, j: (b, c, 0, j, 0))
    s4p = pl.BlockSpec((None, None, 4, nq, LANES), lambda b, c, j: (b, c, 0, prev(j, ATT_SPAN // 4 // nq), 0))
    s16 = pl.BlockSpec((None, None, 16, nq, LANES), lambda b, c, j: (b, c, 0, j, 0))
    s16p = pl.BlockSpec((None, None, 16, nq, LANES), lambda b, c, j: (b, c, 0, prev(j, 1), 0))
    return pl.pallas_call(
        _dilated_body,
        grid=(B, N_SLABS, S // ATT_SPAN),
        in_specs=[s1, s1, s1p, s1, s1p, s4, s4, s4p, s4, s4p, s16, s16, s16p, s16, s16p],
        out_specs=pl.BlockSpec((None, None, ATT_SPAN, LANES), lambda b, c, j: (b, c, j, 0)),
        out_shape=jax.ShapeDtypeStruct((B, N_SLABS, S, LANES), F32),
        scratch_shapes=[pltpu.VMEM((ATT_SPAN + nq, LANES), BF16), pltpu.VMEM((ATT_SPAN + nq, LANES), BF16),
                        pltpu.VMEM((4, ATT_SPAN // 4 + nq, LANES), BF16),
                        pltpu.VMEM((4, ATT_SPAN // 4 + nq, LANES), BF16),
                        pltpu.VMEM((ATT_SPAN, LANES), F32), pltpu.VMEM((ATT_SPAN, LANES), F32),
                        pltpu.VMEM((ATT_SPAN, LANES), F32)],
        compiler_params=_params(("arbitrary", "arbitrary", "arbitrary")),
        name="dilated_attention",
    )(q1, k1, k1, v1, v1, q4, k4, k4, v4, v4, q16, k16, k16, v16, v16)


def _sample_multiplicity(L, WB):
    pad = LANES
    cnt = np.zeros((L, WB + pad), np.float32)
    for l in range(L):
        for window, dil in DILATED_PAIRS:
            for j in range(window // dil + 1):
                idx = WB + l - dil * j
                if idx >= 0:
                    cnt[l, idx] += 1.0
    cnt = np.tile(cnt, (H_ATT, 1))
    return cnt[:, :WB], cnt[:, WB:]


def _sample_att_body(q_ref, kn_ref, vn_ref, bk_ref, bv_ref, cb_ref, cn_ref,
                     o_ref, nk_ref, nv_ref, *, L, WB):
    q = q_ref[...]
    lane_head = lax.broadcasted_iota(jnp.int32, (1, ATT_W), 1) // HD_ATT
    qexp = jnp.concatenate([jnp.where(lane_head == h, q, 0.0) for h in range(H_ATT)], axis=0).astype(BF16)
    zpad = jnp.zeros((LANES - L, ATT_W), F32)
    kn = jnp.concatenate([kn_ref[...], zpad], axis=0).astype(BF16)
    vn = jnp.concatenate([vn_ref[...], zpad], axis=0).astype(BF16)
    bk = bk_ref[...]
    bv = bv_ref[...]
    nt = (((1,), (1,)), ((), ()))
    scale = HD_ATT ** -0.5
    sb = lax.dot_general(qexp, bk.astype(BF16), nt, preferred_element_type=F32) * scale
    sn = lax.dot_general(qexp, kn, nt, preferred_element_type=F32) * scale
    cb = cb_ref[...]
    cn = cn_ref[...]
    sb = jnp.where(cb > 0, sb, -jnp.inf)
    sn = jnp.where(cn > 0, sn, -jnp.inf)
    m = jnp.maximum(jnp.max(sb, axis=-1, keepdims=True), jnp.max(sn, axis=-1, keepdims=True))
    pb = cb * jnp.exp(sb - m)
    pn = cn * jnp.exp(sn - m)
    l = jnp.sum(pb, axis=-1, keepdims=True) + jnp.sum(pn, axis=-1, keepdims=True)
    o_all = (jnp.dot(pb.astype(BF16), bv.astype(BF16), preferred_element_type=F32)
             + jnp.dot(pn.astype(BF16), vn, preferred_element_type=F32)) / l
    o = jnp.zeros((L, ATT_W), F32)
    for h in range(H_ATT):
        o = o + jnp.where(lane_head == h, o_all[h * L:(h + 1) * L], 0.0)
    for c in range(N_SLABS):
        o_ref[c] = o[:, c * LANES:(c + 1) * LANES]
    nk_ref[0:WB - L, :] = bk[L:WB]
    nk_ref[WB - L:WB, :] = kn_ref[...]
    nv_ref[0:WB - L, :] = bv[L:WB]
    nv_ref[WB - L:WB, :] = vn_ref[...]


def _sample_attention(q2d, k2d, v2d, buf_k, buf_v, DB, L):
    WB = buf_k.shape[1]
    cb, cn = _sample_multiplicity(L, WB)
    row = pl.BlockSpec((L, ATT_W), lambda b: (b, 0))
    buf = pl.BlockSpec((None, WB, ATT_W), lambda b: (b, 0, 0))
    const = lambda a: pl.BlockSpec(a.shape, lambda b: (0, 0))
    return pl.pallas_call(
        functools.partial(_sample_att_body, L=L, WB=WB),
        grid=(DB,),
        in_specs=[row, row, row, buf, buf, const(cb), const(cn)],
        out_specs=[pl.BlockSpec((N_SLABS, L, LANES), lambda b: (0, b, 0)), buf, buf],
        out_shape=[jax.ShapeDtypeStruct((N_SLABS, DB * L, LANES), F32),
                   jax.ShapeDtypeStruct((DB, WB, ATT_W), F32),
                   jax.ShapeDtypeStruct((DB, WB, ATT_W), F32)],
        compiler_params=_params(("arbitrary",)),
        name="sample_attention",
    )(q2d, k2d, v2d, buf_k, buf_v, jnp.asarray(cb), jnp.asarray(cn))


def _group_mean(sq, width):
    if width == LANES:
        return jnp.mean(sq, axis=-1, keepdims=True)
    lane = lax.broadcasted_iota(jnp.int32, (1, LANES), 1)
    low = lane < width
    s_lo = jnp.sum(jnp.where(low, sq, 0.0), axis=-1, keepdims=True)
    s_hi = jnp.sum(jnp.where(low, 0.0, sq), axis=-1, keepdims=True)
    return jnp.where(low, s_lo, s_hi) * (1.0 / width)


def _merge_body(or_ref, gr_ref, oa0, oa1, oa2, oa3, x_ref, g1_ref, sh_ref, sc_ref, br_ref, ba_ref, wo_ref,
                n2_ref, rw_ref, rb_ref, cin_ref,
                x1_ref, h2_ref, gt_ref, p8_ref, p8t_ref, meta_ref, cnt_ref, cnt_scr):
    i = pl.program_id(0)

    @pl.when(i == 0)
    def _():
        cnt_scr[...] = cin_ref[...]

    parts = []
    for c in range(RET_V // LANES):
        cs = slice(c * LANES, (c + 1) * LANES)
        o = or_ref[:, cs]
        o = o * lax.rsqrt(_group_mean(o * o, DV_RET) + EPS) * br_ref[:, cs]
        g = gr_ref[:, cs]
        parts.append((o * (g * _sigmoid(g))).astype(BF16))
    for c, oa in enumerate((oa0, oa1, oa2, oa3)):
        cs = slice(c * LANES, (c + 1) * LANES)
        o = oa[...]
        o = o * lax.rsqrt(_group_mean(o * o, HD_ATT) + EPS) * ba_ref[:, cs]
        parts.append(o.astype(BF16))
    mix = jnp.concatenate(parts, axis=1)
    y = jnp.dot(mix, wo_ref[...], preferred_element_type=F32)
    x1 = x_ref[...] + g1_ref[...] * y
    x1_ref[...] = x1
    ms = jnp.mean(x1 * x1, axis=-1, keepdims=True)
    h2 = x1 * lax.rsqrt(ms + EPS) * n2_ref[...]
    h2 = h2 * (1.0 + sc_ref[...]) + sh_ref[...]
    rw = rw_ref[...]
    h_hi = h2.astype(BF16)
    h2_ref[...] = h_hi
    h_lo = (h2 - h_hi.astype(F32)).astype(BF16)
    w_hi = rw.astype(BF16)
    w_lo = (rw - w_hi.astype(F32)).astype(BF16)
    logits = (jnp.dot(h_hi, w_hi, preferred_element_type=F32) + jnp.dot(h_lo, w_hi, preferred_element_type=F32)
              + jnp.dot(h_hi, w_lo, preferred_element_type=F32)) + rb_ref[...]
    tm = logits.shape[0]
    lane = lax.broadcasted_iota(jnp.int32, logits.shape, 1).astype(F32)
    vals, idxs = [], []
    for _ in range(TOP_K):
        m = jnp.max(logits, axis=-1, keepdims=True)
        idx = jnp.min(jnp.where(logits == m, lane, float(N_EXPERTS)), axis=-1, keepdims=True)
        vals.append(m)
        idxs.append(idx)
        logits = jnp.where(lane == idx, -jnp.inf, logits)
    e = [jnp.exp(v - vals[0]) for v in vals]
    tot = e[0] + e[1] + e[2] + e[3]
    for k in range(TOP_K):
        gt_ref[:, k:k + 1] = e[k] / tot
    member = jnp.zeros(logits.shape, F32)
    for k in range(TOP_K):
        member = member + jnp.where(lane == idxs[k], 1.0, 0.0)
    sub = min(TOKEN_TILE, tm)
    ti = lax.broadcasted_iota(jnp.int32, (tm, tm), 0)
    tj = lax.broadcasted_iota(jnp.int32, (tm, tm), 1)
    before = jnp.where((tj < ti) & (tj // sub == ti // sub), 1.0, 0.0).astype(BF16)
    rin = jnp.dot(before, member.astype(BF16), preferred_element_type=F32)
    earlier = jnp.where(lax.broadcasted_iota(jnp.int32, (N_EXPERTS, N_EXPERTS), 0)
                        < lax.broadcasted_iota(jnp.int32, (N_EXPERTS, N_EXPERTS), 1), 1.0, 0.0)
    row_sub = lax.broadcasted_iota(jnp.int32, (tm, 1), 0) // sub
    carry = cnt_scr[...]
    off8 = jnp.zeros(logits.shape, F32)
    for s in range(tm // sub):
        in_sub = row_sub == s
        n = jnp.sum(jnp.where(in_sub, member, 0.0), axis=0, keepdims=True)
        n8 = jnp.floor((n + 7.0) * 0.125) * 8.0
        o8 = jnp.dot(jnp.broadcast_to(n8, (8, N_EXPERTS)), earlier, precision=HIGHEST,
                     preferred_element_type=F32)
        o8 = jnp.max(o8, axis=0, keepdims=True)
        off8 = off8 + jnp.where(in_sub, o8, 0.0)
        meta_ref[s, 0:1, :] = n
        meta_ref[s, 1:2, :] = carry
        carry = carry + n
    lane128 = lax.broadcasted_iota(jnp.int32, (tm, LANES), 1)
    slots_k = jnp.zeros((tm, LANES), F32)
    for k in range(TOP_K):
        pos = jnp.sum(jnp.where(lane == idxs[k], rin + off8, 0.0), axis=-1, keepdims=True)
        p8_ref[:, k:k + 1] = pos.astype(jnp.int32)
        slots_k = slots_k + jnp.where(lane128 == k, pos, 0.0)
    p8t_ref[...] = slots_k.T[0:8, :].astype(jnp.int32)
    cnt_scr[...] = carry
    cnt_ref[...] = carry


def _merge(o_r, g_r, o_a, x2d, g1, sh2, sc2, beta_ret, beta_att, wo_bf, n2, rw, rb, cnt_in, tm,
           tiles_per_group):
    T, D = x2d.shape
    tpg = o_a.shape[2] // tm
    sub = min(TOKEN_TILE, tm)
    row = lambda w: pl.BlockSpec((tm, w), lambda i: (i, 0))
    const = lambda r, c: pl.BlockSpec((r, c), lambda i: (0, 0))
    ms = lambda m: _mod_spec(m, tm, tiles_per_group)
    slab = lambda c: pl.BlockSpec((None, None, tm, LANES), lambda i: (i // tpg, c, i % tpg, 0))
    ins = [o_r, g_r, o_a, o_a, o_a, o_a, x2d, g1, sh2, sc2, beta_ret.reshape(1, RET_V),
           beta_att.reshape(1, ATT_W), wo_bf, n2.reshape(1, D), rw, rb.reshape(1, N_EXPERTS), cnt_in]
    in_specs = [row(RET_V), row(RET_V), slab(0), slab(1), slab(2), slab(3), row(D), ms(g1), ms(sh2), ms(sc2),
                const(1, RET_V), const(1, ATT_W), const(D, D), const(1, D),
                const(D, N_EXPERTS), const(1, N_EXPERTS), const(1, N_EXPERTS)]
    return pl.pallas_call(
        _merge_body,
        grid=(T // tm,),
        in_specs=in_specs,
        out_specs=[row(D), row(D), row(TOP_K), row(TOP_K), pl.BlockSpec((8, tm), lambda i: (0, i)),
                   pl.BlockSpec((tm // sub, 2, N_EXPERTS), lambda i: (i, 0, 0)), const(1, N_EXPERTS)],
        out_shape=[jax.ShapeDtypeStruct((T, D), F32), jax.ShapeDtypeStruct((T, D), BF16),
                   jax.ShapeDtypeStruct((T, TOP_K), F32), jax.ShapeDtypeStruct((T, TOP_K), jnp.int32),
                   jax.ShapeDtypeStruct((8, T), jnp.int32),
                   jax.ShapeDtypeStruct((T // sub, 2, N_EXPERTS), F32),
                   jax.ShapeDtypeStruct((1, N_EXPERTS), F32)],
        scratch_shapes=[pltpu.VMEM((1, N_EXPERTS), F32)],
        compiler_params=_params(("arbitrary",)),
        name="merge_router",
    )(*ins)


def _slots(tile):
    return -(-(tile * TOP_K + N_EXPERTS * SEG) // LANES) * LANES


def _route(counts, meta, n_assign, tm, max_chunks):
    nb = -(-(n_assign + N_EXPERTS * (tm - 1 + SEG - 1)) // tm)
    experts = jnp.arange(N_EXPERTS, dtype=jnp.int32)
    counts = counts.reshape(N_EXPERTS).astype(jnp.int32)
    padded = (counts + (SEG - 1) + tm - 1) // tm * tm
    pad_end = jnp.cumsum(padded)
    pad_start = pad_end - padded
    blk0 = jnp.arange(nb + 1, dtype=jnp.int32) * tm
    blk_exp = jnp.minimum(jnp.sum((pad_end[None, :] <= blk0[:, None]).astype(jnp.int32), axis=1),
                          N_EXPERTS - 1).astype(jnp.int32)
    n_used = (pad_end[-1:] // tm).astype(jnp.int32)
    blk_new = jnp.concatenate([jnp.ones((1,), jnp.int32),
                               (blk_exp[1:] != blk_exp[:-1]).astype(jnp.int32)])
    fill_start = jnp.concatenate([pad_start + counts, pad_end[-1:]]).astype(jnp.int32)
    fill_n = jnp.concatenate([padded - counts, nb * tm - pad_end[-1:]]).astype(jnp.int32)
    n = meta[:, 0, :].astype(jnp.int32)
    start = pad_start[None, :] + meta[:, 1, :].astype(jnp.int32)
    nseg = (n + SEG - 1) // SEG
    cum = jnp.cumsum(nseg, axis=1)
    m = jnp.arange(max_chunks, dtype=jnp.int32)
    e_of = jnp.sum((cum[:, None, :] <= m[None, :, None]).astype(jnp.int32), axis=-1)
    onehot = e_of[:, :, None] == experts[None, None, :]
    first = jnp.sum(jnp.where(onehot, (cum - nseg)[:, None, :], 0), axis=-1)
    st = jnp.sum(jnp.where(onehot, start[:, None, :], 0), axis=-1)
    chunk_row = (st + SEG * (m[None, :] - first)).astype(jnp.int32)
    return blk_exp, blk_new, n_used, fill_start, fill_n, chunk_row, cum[:, -1].astype(jnp.int32), nb


def _alternate_priorities(issue, count):
    def pair(p, carry):
        issue(2 * p, 0)
        issue(2 * p + 1, 1)
        return carry

    lax.fori_loop(0, count // 2, pair, 0)

    @pl.when(count % 2 == 1)
    def _():
        issue(count - 1, 0)


def _wait_copies(count, sem, src, dst, rows_per_copy, max_count):
    sz = 1 << (max_count.bit_length() - 1)
    while sz >= 1:
        @pl.when((count & sz) != 0)
        def _(sz=sz):
            pltpu.make_async_copy(src.at[pl.ds(0, sz * rows_per_copy)], dst.at[pl.ds(0, sz * rows_per_copy)],
                                  sem).wait()
        sz //= 2


def _dispatch_body(rows_ref, tot_ref, fs_ref, fn_ref, p8t_ref, hp_ref, hs_ref, xs_hbm,
                   sbuf, zbuf, sem, zsem, *, tm, slots, max_chunks, fill_max, prompt_tiles):
    i = pl.program_id(0)
    slot = i % 2
    seg_rows = SEG * ROW_CHUNKS

    h = jnp.where(i < prompt_tiles, hp_ref[...], hs_ref[...])
    s_iota = lax.broadcasted_iota(jnp.int32, (slots, tm), 0)
    p = jnp.zeros((slots, tm), F32)
    for k in range(TOP_K):
        p = p + jnp.where(s_iota == p8t_ref[k:k + 1, :], 1.0, 0.0)
    srt = jnp.dot(p.astype(BF16), h, preferred_element_type=F32)
    for c in range(ROW_CHUNKS):
        sbuf[slot, pl.ds(c, slots, stride=ROW_CHUNKS), :] = srt[:, c * LANES:(c + 1) * LANES]

    @pl.when(i > 0)
    def _():
        _wait_copies(tot_ref[i - 1], sem.at[1 - slot], sbuf.at[1 - slot], xs_hbm, seg_rows, max_chunks)

    def one(m, priority):
        row = pl.multiple_of(rows_ref[i * max_chunks + m] * ROW_CHUNKS, ROW_CHUNKS)
        src = pl.multiple_of(m * seg_rows, seg_rows)
        pltpu.make_async_copy(sbuf.at[slot, pl.ds(src, seg_rows)], xs_hbm.at[pl.ds(row, seg_rows)],
                              sem.at[slot]).start(priority)

    _alternate_priorities(one, tot_ref[i])

    @pl.when(i == pl.num_programs(0) - 1)
    def _():
        _wait_copies(tot_ref[i], sem.at[slot], sbuf.at[slot], xs_hbm, seg_rows, max_chunks)
        zbuf[...] = jnp.zeros(zbuf.shape, F32)

        def fill(e, wait):
            start = fs_ref[e]
            n = fn_ref[e]
            sz = fill_max
            while sz >= 1:
                @pl.when((n & sz) != 0)
                def _(start=start, sz=sz):
                    cp = pltpu.make_async_copy(
                        zbuf.at[pl.ds(0, sz * ROW_CHUNKS)],
                        xs_hbm.at[pl.ds(pl.multiple_of(start * ROW_CHUNKS, ROW_CHUNKS), sz * ROW_CHUNKS)], zsem)
                    cp.wait() if wait else cp.start()
                start = start + (n & sz)
                sz //= 2

        def fill_all(wait):
            def body(e, carry):
                fill(e, wait)
                return carry
            lax.fori_loop(0, N_EXPERTS, body, 0)

            def tail(c, carry):
                row = pl.multiple_of((fs_ref[N_EXPERTS] + c * fill_max) * ROW_CHUNKS, ROW_CHUNKS)
                cp = pltpu.make_async_copy(zbuf, xs_hbm.at[pl.ds(row, fill_max * ROW_CHUNKS)], zsem)
                cp.wait() if wait else cp.start()
                return carry
            lax.fori_loop(0, fn_ref[N_EXPERTS] // fill_max, tail, 0)

        fill_all(False)
        fill_all(True)


def _dispatch(h2_p, h2_s, p8t, chunk_row, tot, fill_start, fill_n, R, tm, fill_max):
    Tp, D = h2_p.shape
    Ts = h2_s.shape[0]
    np_, ns_ = Tp // tm, Ts // tm
    slots = _slots(tm)
    max_chunks = slots // SEG
    grid_spec = pltpu.PrefetchScalarGridSpec(
        num_scalar_prefetch=4,
        grid=(np_ + ns_,),
        in_specs=[pl.BlockSpec((8, tm), lambda i, *_: (0, i)),
                  pl.BlockSpec((tm, D), lambda i, *_: (jnp.minimum(i, np_ - 1), 0)),
                  pl.BlockSpec((tm, D), lambda i, *_: (jnp.maximum(i - np_, 0), 0))],
        out_specs=pl.BlockSpec(memory_space=pl.ANY),
        scratch_shapes=[pltpu.VMEM((2, slots * ROW_CHUNKS, LANES), F32),
                        pltpu.VMEM((fill_max * ROW_CHUNKS, LANES), F32),
                        pltpu.SemaphoreType.DMA((2,)), pltpu.SemaphoreType.DMA(())],
    )
    return pl.pallas_call(
        functools.partial(_dispatch_body, tm=tm, slots=slots, max_chunks=max_chunks, fill_max=fill_max,
                          prompt_tiles=np_),
        grid_spec=grid_spec,
        out_shape=jax.ShapeDtypeStruct((R * ROW_CHUNKS, LANES), F32),
        compiler_params=_params(("arbitrary",)),
        name="moe_dispatch",
    )(chunk_row.reshape(-1), tot, fill_start, fill_n, p8t, h2_p, h2_s)


def _moe_body(nused_ref, bexp_ref, new_ref, x_ref, wu_ref, bu_ref, wd_ref, bd_ref, y_ref, wu_bf, wd_bf):
    i = pl.program_id(0)

    @pl.when(new_ref[i] != 0)
    def _():
        wu_bf[...] = wu_ref[...].astype(BF16)
        wd_bf[...] = wd_ref[...].astype(BF16)

    @pl.when(i < nused_ref[0])
    def _():
        tm = x_ref.shape[0] // ROW_CHUNKS
        x = jnp.concatenate([x_ref[pl.ds(c, tm, stride=ROW_CHUNKS), :].astype(BF16)
                             for c in range(ROW_CHUNKS)], axis=1)
        u = jnp.dot(x, wu_bf[...], preferred_element_type=F32) + bu_ref[...]
        x_glu = jnp.minimum(u[:, :D_FF], SWIGLU_LIMIT)
        x_lin = jnp.clip(u[:, D_FF:], -SWIGLU_LIMIT, SWIGLU_LIMIT)
        act = x_glu * _sigmoid(SWIGLU_ALPHA * x_glu) * (x_lin + 1.0)
        y = jnp.dot(act.astype(BF16), wd_bf[...], preferred_element_type=F32) + bd_ref[...]
        for c in range(ROW_CHUNKS):
            y_ref[pl.ds(c, tm, stride=ROW_CHUNKS), :] = y[:, c * LANES:(c + 1) * LANES]

    @pl.when(i >= nused_ref[0])
    def _():
        y_ref[...] = jnp.zeros(y_ref.shape, F32)


def _moe_experts(xs, blk_exp, blk_new, n_used, w_up, bu, w_down, bd, tm):
    R, D = xs.shape[0] // ROW_CHUNKS, D_MODEL
    blk = lambda i, nu, be, nw: (jnp.minimum(i, nu[0] - 1), 0)
    wspec = lambda r, c: pl.BlockSpec((None, r, c), lambda i, nu, be, nw: (be[i], 0, 0))
    grid_spec = pltpu.PrefetchScalarGridSpec(
        num_scalar_prefetch=3,
        grid=(R // tm + 1,),
        in_specs=[pl.BlockSpec((tm * ROW_CHUNKS, LANES), blk), wspec(D, 2 * D_FF), wspec(1, 2 * D_FF),
                  wspec(D_FF, D), wspec(1, D)],
        out_specs=pl.BlockSpec((tm * ROW_CHUNKS, LANES), lambda i, nu, be, nw: (i, 0)),
        scratch_shapes=[pltpu.VMEM((D, 2 * D_FF), BF16), pltpu.VMEM((D_FF, D), BF16)],
    )
    return pl.pallas_call(
        _moe_body,
        grid_spec=grid_spec,
        out_shape=jax.ShapeDtypeStruct(((R + tm) * ROW_CHUNKS, LANES), F32),
        compiler_params=_params(("arbitrary",)),
        name="moe_experts",
    )(n_used, blk_exp, blk_new, xs, w_up, bu.reshape(N_EXPERTS, 1, 2 * D_FF), w_down,
      bd.reshape(N_EXPERTS, 1, D))


def _combine_body(rows_ref, tot_ref, yb_hbm, p8_ref, gate_ref, x1_ref, g2_ref, shf_ref, scf_ref, gf_ref,
                  out_ref, sbuf, sem, *, tm, slots, max_chunks):
    i = pl.program_id(0)
    nsteps = pl.num_programs(0)
    seg_rows = SEG * ROW_CHUNKS

    @pl.when(i == 0)
    def _():
        sbuf[...] = jnp.zeros(sbuf.shape, F32)

    def start_runs(tile, slot):
        def one(m, priority):
            src = pl.multiple_of(rows_ref[tile * max_chunks + m] * ROW_CHUNKS, ROW_CHUNKS)
            dst = pl.multiple_of(m * seg_rows, seg_rows)
            pltpu.make_async_copy(yb_hbm.at[pl.ds(src, seg_rows)], sbuf.at[slot, pl.ds(dst, seg_rows)],
                                  sem.at[slot]).start(priority)
        _alternate_priorities(one, tot_ref[tile])

    @pl.when(i == 0)
    def _():
        start_runs(0, 0)

    @pl.when(i + 1 < nsteps)
    def _():
        start_runs(i + 1, (i + 1) % 2)

    slot = i % 2
    _wait_copies(tot_ref[i], sem.at[slot], yb_hbm, sbuf.at[slot], seg_rows, max_chunks)
    pos = p8_ref[...]
    gates = gate_ref[...]
    s_iota = lax.broadcasted_iota(jnp.int32, (tm, slots), 1)
    g = jnp.zeros((tm, slots), F32)
    for k in range(TOP_K):
        g = g + jnp.where(s_iota == pos[:, k:k + 1], gates[:, k:k + 1], 0.0)
    gb = g.astype(BF16)
    ys = []
    for c in range(ROW_CHUNKS):
        chunk = sbuf[slot, pl.ds(c, slots, stride=ROW_CHUNKS), :]
        ys.append(jnp.dot(gb, chunk.astype(BF16), preferred_element_type=F32))
    y = jnp.concatenate(ys, axis=1)
    x2 = x1_ref[...] + g2_ref[...] * y
    ms = jnp.mean(x2 * x2, axis=-1, keepdims=True)
    h = x2 * lax.rsqrt(ms + EPS) * gf_ref[...]
    out_ref[...] = h * (1.0 + scf_ref[...]) + shf_ref[...]


def _combine(chunk_row, tot, yb, pos8, gates, x1, g2, shf, scf, gf, tm, tiles_per_group):
    T, D = x1.shape
    nsteps = T // tm
    slots = _slots(tm)
    max_chunks = slots // SEG
    row = lambda w: pl.BlockSpec((tm, w), lambda i, rows, tot: (i, 0))
    ms = lambda m: pl.BlockSpec((None, m.shape[1], m.shape[2]),
                                lambda i, rows, tot: (i // tiles_per_group, 0, 0))
    grid_spec = pltpu.PrefetchScalarGridSpec(
        num_scalar_prefetch=2,
        grid=(nsteps,),
        in_specs=[pl.BlockSpec(memory_space=pl.ANY), row(TOP_K), row(TOP_K), row(D), ms(g2), ms(shf), ms(scf),
                  pl.BlockSpec((1, D), lambda i, rows, tot: (0, 0))],
        out_specs=row(D),
        scratch_shapes=[pltpu.VMEM((2, slots * ROW_CHUNKS, LANES), F32), pltpu.SemaphoreType.DMA((2,))],
    )
    return pl.pallas_call(
        functools.partial(_combine_body, tm=tm, slots=slots, max_chunks=max_chunks),
        grid_spec=grid_spec,
        out_shape=jax.ShapeDtypeStruct((T, D), F32),
        compiler_params=_params(("arbitrary",)),
        name="combine_final",
    )(chunk_row.reshape(-1), tot, yb, pos8, gates, x1, g2, shf, scf, gf.reshape(1, D))


def _group_mods(mod, n, rows_per_group, per_row):
    outs = []
    for j in range(n):
        m = mod[:, j * D_MODEL:(j + 1) * D_MODEL]
        if per_row:
            outs.append(jnp.repeat(m, rows_per_group, axis=0)[None])
        else:
            outs.append(m[:, None, :])
    return outs


def kernel(x_prompt, x_sample, state_ret, cache_win_k, cache_win_v, c_prompt, c_sample,
           ada_w, ada_b, norm1_g, w_in, beta_ret, beta_att, w_out, norm2_g,
           router_w, router_b, w_up, b_up, w_down, b_down,
           final_ada_w, final_ada_b, final_norm_g):
    B, S, D = x_prompt.shape
    DB, L = x_sample.shape[:2]
    WB = cache_win_k.shape[2]
    Tp, Ts = B * S, DB * L
    assert ada_w.shape[0] == 1, "single-layer trunk"
    c_all = jnp.concatenate([c_prompt, c_sample], axis=0)
    c_all = jnp.pad(c_all, ((0, (-c_all.shape[0]) % 8), (0, 0)))
    mods = _adaln(c_all, ada_w[0], ada_b[0])
    fmods = _adaln(c_all, final_ada_w, final_ada_b)
    w_in_bf, w_out_bf = w_in[0].astype(BF16), w_out[0].astype(BF16)

    tmp = min(ROW_TILE, S)
    tms = Ts
    assert Tp % tmp == 0
    mp = _group_mods(mods[:B], N_MOD, S, False)
    msm = _group_mods(mods[B:B + DB], N_MOD, L, True)
    fp = _group_mods(fmods[:B], 2, S, False)
    fs = _group_mods(fmods[B:B + DB], 2, L, True)

    cos_p, sin_p = _rope_tables(jnp.arange(S))
    outs = _inproj(x_prompt.reshape(Tp, D), mp[0], mp[1], norm1_g[0], w_in_bf, cos_p, sin_p,
                   tmp, S // tmp, True, B)
    qr, kr, vr, gr, ka, va = outs[:6]
    o_r_p, s_fin = _retention(qr, kr, vr, jnp.zeros((B, H_RET, DK_RET, DV_RET), F32), B, S, RET_CHUNK)
    o_a_p = _dilated_prompt(outs[6:], B, S)
    keep = min(MAX_WINDOW, S)
    win = lambda t: t.reshape(B, keep, H_ATT, HD_ATT)
    wk_p, wv_p = win(ka), win(va)

    cos_s, sin_s = _rope_tables(PAST_LEN + jnp.arange(L))
    cos_s, sin_s = jnp.tile(cos_s, (DB, 1)), jnp.tile(sin_s, (DB, 1))
    qr_s, kr_s, vr_s, gr_s, qa_s, ka_s, va_s = _inproj(
        x_sample.reshape(Ts, D), msm[0], msm[1], norm1_g[0], w_in_bf, cos_s, sin_s, tms, 1, False)
    o_r_s, s_new = _retention(qr_s, kr_s, vr_s, state_ret[0], DB, L, L)
    o_a_s, nk, nv = _sample_attention(qa_s, ka_s, va_s, cache_win_k[0].reshape(DB, WB, ATT_W),
                                      cache_win_v[0].reshape(DB, WB, ATT_W), DB, L)

    T_all = Tp + Ts
    zero_cnt = jnp.zeros((1, N_EXPERTS), F32)
    x1_p, h2_p, gt_p, p8_p, p8t_p, meta_p, cnt_p = _merge(
        o_r_p, gr, o_a_p, x_prompt.reshape(Tp, D), mp[2], mp[3], mp[4], beta_ret[0], beta_att[0], w_out_bf,
        norm2_g[0], router_w[0], router_b[0], zero_cnt, tmp, S // tmp)
    x1_s, h2_s, gt_s, p8_s, p8t_s, meta_s, cnt = _merge(
        o_r_s, gr_s, o_a_s[None], x_sample.reshape(Ts, D), msm[2], msm[3], msm[4], beta_ret[0], beta_att[0],
        w_out_bf, norm2_g[0], router_w[0], router_b[0], cnt_p, tms, 1)

    tile = min(TOKEN_TILE, tmp)
    assert tile == min(TOKEN_TILE, tms) and tile == Ts, "both groups need the same token tile"
    tme = min(MOE_TILE, T_all)
    meta = jnp.concatenate([meta_p, meta_s], axis=0)
    blk_exp, blk_new, n_used, fill_start, fill_n, chunk_row, tot, nb = _route(
        cnt, meta, T_all * TOP_K, tme, _slots(tile) // SEG)
    fill_max = 1 << (tme + SEG - 2).bit_length() >> 1
    xs = _dispatch(h2_p, h2_s, jnp.concatenate([p8t_p, p8t_s], axis=1), chunk_row, tot, fill_start, fill_n,
                   nb * tme, tile, fill_max)
    yb = _moe_experts(xs, blk_exp, blk_new, n_used, w_up[0], b_up[0], w_down[0], b_down[0], tme)

    tp = Tp // tile
    y_p = _combine(chunk_row[:tp], tot[:tp], yb, p8_p, gt_p, x1_p, mp[5], fp[0], fp[1], final_norm_g,
                   tile, S // tile)
    y_s = _combine(chunk_row[tp:], tot[tp:], yb, p8_s, gt_s, x1_s, msm[5], fs[0], fs[1], final_norm_g,
                   tile, 1)

    shp = (DB, WB, H_ATT, HD_ATT)
    return (y_p.reshape(B, S, D), y_s.reshape(DB, L, D), s_fin[None], s_new[None],
            wk_p[None], wv_p[None], nk.reshape(shp)[None], nv.reshape(shp)[None])
```
